```python
import jax, jax.numpy as jnp
from jax import lax
import numpy as np

D_MODEL = 1024
BATCH = 16
SEQ = 4096
DEPTH = 4
DEC_BATCH = 32
DEC_SEQ = 32
PAST_LEN = 4096

CHUNK = 64
MIX_WIDTH = D_MODEL
RET_WIDTH = MIX_WIDTH // 2
RET_HEADS = 4
RET_DK = RET_WIDTH // RET_HEADS
RET_DV = RET_WIDTH // RET_HEADS
GM_WIDTH = MIX_WIDTH - RET_WIDTH
GM_GROUPS = 4
GM_GROUP_DIM = GM_WIDTH // GM_GROUPS
GM_CHUNK = 128
D_FF = 4 * D_MODEL
IN_WIDTH = 4 * RET_WIDTH + 2 * GM_WIDTH
SPLITS = (RET_WIDTH, 2 * RET_WIDTH, 3 * RET_WIDTH, 4 * RET_WIDTH, 4 * RET_WIDTH + GM_WIDTH)
ROPE_BASE = 10000.0
EPS = 1e-6

kernel_name = 'hybrid_retention_gmlp_stream_step'


def _rmsnorm(x, g):
    xf = x.astype(jnp.float32)
    y = xf * lax.rsqrt(jnp.mean(xf * xf, axis=-1, keepdims=True) + EPS)
    return (y * g.astype(jnp.float32)).astype(x.dtype)


def _layernorm(x, g, b):
    xf = x.astype(jnp.float32)
    mu = jnp.mean(xf, axis=-1, keepdims=True)
    xc = xf - mu
    y = xc * lax.rsqrt(jnp.mean(xc * xc, axis=-1, keepdims=True) + EPS)
    return (y * g.astype(jnp.float32) + b.astype(jnp.float32)).astype(x.dtype)


def _rope(x, pos):
    half = x.shape[-1] // 2
    inv_freq = ROPE_BASE ** (-jnp.arange(half, dtype=jnp.float32) / half)
    ang = pos[:, None] * inv_freq[None, :]
    cos = jnp.cos(ang)[None, :, None, :]
    sin = jnp.sin(ang)[None, :, None, :]
    xf = x.astype(jnp.float32)
    x1, x2 = xf[..., :half], xf[..., half:]
    return jnp.concatenate([x1 * cos - x2 * sin, x2 * cos + x1 * sin], axis=-1)


def _log_gammas():
    return jnp.log(1.0 - jnp.exp2(-5.0 - jnp.arange(RET_HEADS, dtype=jnp.float32)))


def _retention_block(q, k, v, s):
    c = q.shape[1]
    lg = _log_gammas()
    idx = jnp.arange(c, dtype=jnp.float32)
    diff = idx[:, None] - idx[None, :]
    decay = jnp.where(diff >= 0, jnp.exp(lg[:, None, None] * jnp.maximum(diff, 0.0)), 0.0)
    scores = jnp.einsum('bihd,bjhd->bhij', q, k) * decay[None]
    o = jnp.einsum('bhij,bjhe->bihe', scores, v)
    cross = jnp.exp(lg[None, :] * (idx[:, None] + 1.0))
    o = o + jnp.einsum('bihd,bhde->bihe', q, s) * cross[None, :, :, None]
    kdec = jnp.exp(lg[None, :] * (c - 1.0 - idx[:, None]))
    s_new = s * jnp.exp(lg * c)[None, :, None, None] + jnp.einsum('bjhd,jh,bjhe->bhde', k, kdec, v)
    return o, s_new


def _retention_prompt(q, k, v):
    b, s, h, _ = q.shape
    n = s // CHUNK

    def to_chunks(t):
        return t.reshape(b, n, CHUNK, h, t.shape[-1]).swapaxes(0, 1)

    def step(state, qkv):
        o, state_new = _retention_block(qkv[0], qkv[1], qkv[2], state)
        return state_new, o

    s0 = jnp.zeros((b, h, RET_DK, RET_DV), jnp.float32)
    s_fin, o = lax.scan(step, s0, (to_chunks(q), to_chunks(k), to_chunks(v)))
    return o.swapaxes(0, 1).reshape(b, s, h, RET_DV), s_fin


def _layer(x, pos, s_in, g_mix, w_in, ret_gn, gm_ln_g, gm_ln_b, gm_w, gm_b, w_out, g_ffn, w_up, w_down):
    b, s, _ = x.shape
    h = _rmsnorm(x, g_mix)
    proj = h @ w_in
    q, k, v, gate, u, gv = jnp.split(proj, SPLITS, axis=-1)
    q = _rope(q.reshape(b, s, RET_HEADS, RET_DK), pos) * (RET_DK ** -0.5)
    k = _rope(k.reshape(b, s, RET_HEADS, RET_DK), pos)
    v = v.reshape(b, s, RET_HEADS, RET_DV).astype(jnp.float32)
    if s_in is None:
        o, s_out = _retention_prompt(q, k, v)
    else:
        o, s_out = _retention_block(q, k, v, s_in.astype(jnp.float32))
    o = o * lax.rsqrt(jnp.mean(o * o, axis=-1, keepdims=True) + EPS)
    ret_out = (jax.nn.silu(gate.astype(jnp.float32)) * o.reshape(b, s, RET_WIDTH)
               * ret_gn.astype(jnp.float32)).astype(x.dtype)
    u = jax.nn.gelu(u)
    gv = _layernorm(jax.nn.gelu(gv), gm_ln_g, gm_ln_b)
    L = min(s, GM_CHUNK)
    vc = gv.reshape(b, s // L, L, GM_GROUPS, GM_GROUP_DIM)
    w_s = jnp.tril(gm_w[:, :L, :L])
    z = jnp.einsum('gij,bnjgd->bnigd', w_s, vc) + gm_b[:, :L].T[None, None, :, :, None]
    gm_out = u * z.reshape(b, s, GM_WIDTH)
    x = x + jnp.concatenate([ret_out, gm_out], axis=-1) @ w_out
    h2 = _rmsnorm(x, g_ffn)
    x = x + jnp.square(jax.nn.relu(h2 @ w_up)) @ w_down
    return x, s_out, gv


def setup_inputs(seed: int = 0) -> dict:
    key = jax.random.key(seed)
    ks = jax.random.split(key, 16)
    f32 = jnp.float32

    def nrm(k, shape, scale):
        return jax.random.normal(k, shape, f32) * scale

    return {
        'x_prompt': nrm(ks[0], (BATCH, SEQ, D_MODEL), 1.0),
        'x_sample': nrm(ks[1], (DEC_BATCH, DEC_SEQ, D_MODEL), 1.0),
        'state_ret': nrm(ks[2], (DEPTH, DEC_BATCH, RET_HEADS, RET_DK, RET_DV), 1.0),
        'g_mix': 1.0 + nrm(ks[3], (DEPTH, D_MODEL), 0.02),
        'w_in': nrm(ks[4], (DEPTH, D_MODEL, IN_WIDTH), D_MODEL ** -0.5),
        'ret_gn': 1.0 + nrm(ks[5], (DEPTH, RET_WIDTH), 0.02),
        'gm_ln_g': 1.0 + nrm(ks[6], (DEPTH, GM_WIDTH), 0.02),
        'gm_ln_b': nrm(ks[7], (DEPTH, GM_WIDTH), 0.02),
        'gm_w': nrm(ks[8], (DEPTH, GM_GROUPS, GM_CHUNK, GM_CHUNK), GM_CHUNK ** -0.5),
        'gm_b': 1.0 + nrm(ks[9], (DEPTH, GM_GROUPS, GM_CHUNK), 0.02),
        'w_out': nrm(ks[10], (DEPTH, MIX_WIDTH, D_MODEL), MIX_WIDTH ** -0.5),
        'g_ffn': 1.0 + nrm(ks[11], (DEPTH, D_MODEL), 0.02),
        'w_up': nrm(ks[12], (DEPTH, D_MODEL, D_FF), D_MODEL ** -0.5),
        'w_down': nrm(ks[13], (DEPTH, D_FF, D_MODEL), D_FF ** -0.5),
        'g_final': 1.0 + nrm(ks[14], (D_MODEL,), 0.02),
    }


def reference(x_prompt, x_sample, state_ret, g_mix, w_in, ret_gn, gm_ln_g, gm_ln_b, gm_w, gm_b,
              w_out, g_ffn, w_up, w_down, g_final):
    pos_p = jnp.arange(x_prompt.shape[1], dtype=jnp.float32)
    pos_s = PAST_LEN + jnp.arange(x_sample.shape[1], dtype=jnp.float32)
    hp, hs = x_prompt, x_sample
    sp, ss, gvs = [], [], []
    for l in range(DEPTH):
        w = (g_mix[l], w_in[l], ret_gn[l], gm_ln_g[l], gm_ln_b[l], gm_w[l], gm_b[l],
             w_out[l], g_ffn[l], w_up[l], w_down[l])
        hp, s_p, _ = _layer(hp, pos_p, None, *w)
        hs, s_s, gv_s = _layer(hs, pos_s, state_ret[l], *w)
        sp.append(s_p)
        ss.append(s_s)
        gvs.append(gv_s)
    y_prompt = _rmsnorm(hp, g_final)
    y_sample = _rmsnorm(hs, g_final)
    new_state_ret_prompt = jnp.stack(sp)
    new_state_ret_sample = jnp.stack(ss)
    new_gm_v_sample = jnp.stack(gvs)
    return (y_prompt, y_sample, new_state_ret_prompt, new_state_ret_sample, new_gm_v_sample)
```

```python
import functools
import math

import jax
import jax.numpy as jnp
from jax import lax
from jax.experimental import pallas as pl
from jax.experimental.pallas import tpu as pltpu

D_MODEL = 1024
PAST_LEN = 4096
RET_HEADS = 4
HEAD_DIM = 128
RET_WIDTH = RET_HEADS * HEAD_DIM
GM_GROUPS = 4
GM_WIDTH = GM_GROUPS * HEAD_DIM
GM_CHUNK = 128
D_FF = 4 * D_MODEL
ROPE_BASE = 10000.0
EPS = 1e-6
Q_SCALE = HEAD_DIM ** -0.5

_OFF_Q, _OFF_K, _OFF_V, _OFF_GATE, _OFF_U, _OFF_GV = (i * RET_WIDTH for i in range(6))

_VMEM_LIMIT_BYTES = 56 * 1024 * 1024
_FFN_COL_CHUNK = 1024

_BF16 = jnp.bfloat16
_F32 = jnp.float32


def _rms_scale(x, g):
    ms = jnp.mean(x * x, axis=-1, keepdims=True)
    return x * lax.rsqrt(ms + EPS) * g


def _gelu_tanh(x):
    c = math.sqrt(2.0 / math.pi)
    return 0.5 * x * (1.0 + jnp.tanh(c * (x + 0.044715 * (x * x * x))))


def _dot(a, b):
    return jnp.dot(a, b, preferred_element_type=_F32)


def _mixer_kernel(*refs, nb, T, C, L, has_state, want_gv):
    it = iter(refs)
    x_ref = next(it)
    s0_ref = next(it) if has_state else None
    (gmix_ref, win_ref, cos_ref, sin_ref, decay_ref, cross_ref, kdec_ref, retgn_ref,
     lng_ref, lnb_ref, gmw_ref, gmb_ref, wout_ref) = (next(it) for _ in range(13))
    xo_ref = next(it)
    so_ref = next(it)
    gv_ref = next(it) if want_gv else None
    q_s, k_s, v_s, gate_s, u_s, gvb_s, mix_s = (next(it) for _ in range(7))

    t = pl.program_id(1)

    @pl.when(t == 0)
    def _():
        if has_state:
            so_ref[...] = s0_ref[...]
        else:
            so_ref[...] = jnp.zeros_like(so_ref)

    x = x_ref[...]
    h = _rms_scale(x, gmix_ref[...]).astype(_BF16)

    def proj(off):
        return _dot(h, win_ref[:, off:off + RET_WIDTH])

    cos = cos_ref[...]
    sin = sin_ref[...]

    def rope_to(dst, p):
        for hd in range(RET_HEADS):
            sl = slice(hd * HEAD_DIM, (hd + 1) * HEAD_DIM)
            ph = p[:, sl]
            dst[:, sl] = (ph * cos + pltpu.roll(ph, HEAD_DIM // 2, 1) * sin).astype(_BF16)

    gate = proj(_OFF_GATE)
    gate_s[...] = gate * jax.nn.sigmoid(gate) * retgn_ref[...]
    v_s[...] = proj(_OFF_V).astype(_BF16)
    rope_to(k_s, proj(_OFF_K))
    rope_to(q_s, proj(_OFF_Q))

    n_chunks = T // C
    for i in range(nb):
        for c in range(n_chunks):
            rows = slice(i * T + c * C, i * T + (c + 1) * C)
            for hd in range(RET_HEADS):
                sl = slice(hd * HEAD_DIM, (hd + 1) * HEAD_DIM)
                qh = q_s[rows, sl]
                kh = k_s[rows, sl]
                vh = v_s[rows, sl]
                state = so_ref[i, hd]
                sc = lax.dot_general(qh, kh, (((1,), (1,)), ((), ())),
                                     preferred_element_type=_F32)
                sc = (sc * decay_ref[hd]).astype(_BF16)
                o = _dot(sc, vh) + _dot(qh, state.astype(_BF16)) * cross_ref[hd]
                kd = (kh.astype(_F32) * kdec_ref[hd]).astype(_BF16)
                kv = lax.dot_general(kd, vh, (((0,), (0,)), ((), ())),
                                     preferred_element_type=_F32)
                so_ref[i, hd] = state * math.exp(_log_gamma(hd) * C) + kv
                on = o * lax.rsqrt(jnp.mean(o * o, axis=-1, keepdims=True) + EPS)
                mix_s[rows, sl] = (gate_s[rows, sl] * on).astype(_BF16)

    u_s[...] = _gelu_tanh(proj(_OFF_U))
    gvp = _gelu_tanh(proj(_OFF_GV))
    mu = jnp.mean(gvp, axis=-1, keepdims=True)
    gc = gvp - mu
    gv = gc * lax.rsqrt(jnp.mean(gc * gc, axis=-1, keepdims=True) + EPS)
    gv = gv * lng_ref[...] + lnb_ref[...]
    if want_gv:
        gv_ref[...] = gv
    gvb_s[...] = gv.astype(_BF16)

    row_id = lax.broadcasted_iota(jnp.int32, (L, L), 0)
    col_id = lax.broadcasted_iota(jnp.int32, (L, L), 1)
    causal = row_id >= col_id
    n_gchunks = T // L
    for g in range(GM_GROUPS):
        sl = slice(g * HEAD_DIM, (g + 1) * HEAD_DIM)
        osl = slice(RET_WIDTH + g * HEAD_DIM, RET_WIDTH + (g + 1) * HEAD_DIM)
        ws = jnp.where(causal, gmw_ref[g], 0.0).astype(_BF16)
        bias = gmb_ref[g]
        for i in range(nb):
            for c in range(n_gchunks):
                rows = slice(i * T + c * L, i * T + (c + 1) * L)
                z = _dot(ws, gvb_s[rows, sl]) + bias
                mix_s[rows, osl] = (u_s[rows, sl] * z).astype(_BF16)

    xo_ref[...] = x + _dot(mix_s[...], wout_ref[...])


def _ffn_kernel(x_ref, g_ref, wup_ref, wdn_ref, gfin_ref, o_ref, hid_s, *, final):
    x = x_ref[...]
    h = _rms_scale(x, g_ref[...]).astype(_BF16)
    for j in range(D_FF // _FFN_COL_CHUNK):
        cols = slice(j * _FFN_COL_CHUNK, (j + 1) * _FFN_COL_CHUNK)
        a = jnp.maximum(_dot(h, wup_ref[:, cols]), 0.0)
        hid_s[:, cols] = (a * a).astype(_BF16)
    y = x + _dot(hid_s[...], wdn_ref[...])
    if final:
        y = _rms_scale(y, gfin_ref[...])
    o_ref[...] = y


def _log_gamma(hd):
    return math.log(1.0 - 2.0 ** (-5.0 - hd))


def _retention_constants(C):
    lg = jnp.asarray([_log_gamma(hd) for hd in range(RET_HEADS)], _F32)
    idx = jnp.arange(C, dtype=_F32)
    diff = idx[:, None] - idx[None, :]
    decay = jnp.where(diff >= 0, jnp.exp(lg[:, None, None] * jnp.maximum(diff, 0.0)), 0.0)
    cross = jnp.exp(lg[:, None] * (idx[None, :] + 1.0))
    kdec = jnp.exp(lg[:, None] * (C - 1.0 - idx[None, :]))
    bcast = lambda a: jnp.broadcast_to(a[:, :, None], (RET_HEADS, C, HEAD_DIM))
    return decay * Q_SCALE, bcast(cross * Q_SCALE), bcast(kdec)


def _rope_tables(pos):
    half = HEAD_DIM // 2
    inv_freq = ROPE_BASE ** (-jnp.arange(half, dtype=_F32) / half)
    ang = pos[:, None] * inv_freq[None, :]
    cos, sin = jnp.cos(ang), jnp.sin(ang)
    return jnp.concatenate([cos, cos], axis=-1), jnp.concatenate([-sin, sin], axis=-1)


def _const_spec(shape):
    return pl.BlockSpec(shape, lambda b, t: (0,) * len(shape), pipeline_mode=pl.Buffered(1))


def _mixer(x2d, s0, lw, tables, *, n_streams, S, nb, T, C, L, want_gv):
    has_state = s0 is not None
    assert S % T == 0 and T % C == 0 and T % L == 0 and n_streams % nb == 0
    assert nb == 1 or S == T
    R = nb * T
    n_t = S // T
    cos, sin, decay, cross, kdec = tables
    gmw = lw['gm_w'][:, :L, :L]
    gmb = jnp.broadcast_to(lw['gm_b'][:, :L, None], (GM_GROUPS, L, HEAD_DIM))

    row_spec = lambda width: pl.BlockSpec((R, width), lambda b, t: (b * n_t + t, 0))
    state_spec = pl.BlockSpec((nb, RET_HEADS, HEAD_DIM, HEAD_DIM), lambda b, t: (b, 0, 0, 0))

    in_specs = [row_spec(D_MODEL)]
    args = [x2d]
    if has_state:
        in_specs.append(state_spec)
        args.append(s0)
    in_specs += [
        _const_spec((1, D_MODEL)),
        _const_spec((D_MODEL, 6 * RET_WIDTH)),
        pl.BlockSpec((R, HEAD_DIM), lambda b, t: (t, 0)),
        pl.BlockSpec((R, HEAD_DIM), lambda b, t: (t, 0)),
        _const_spec((RET_HEADS, C, C)),
        _const_spec((RET_HEADS, C, HEAD_DIM)),
        _const_spec((RET_HEADS, C, HEAD_DIM)),
        _const_spec((1, RET_WIDTH)),
        _const_spec((1, GM_WIDTH)),
        _const_spec((1, GM_WIDTH)),
        _const_spec((GM_GROUPS, L, L)),
        _const_spec((GM_GROUPS, L, HEAD_DIM)),
        _const_spec((D_MODEL, D_MODEL)),
    ]
    args += [lw['g_mix'], lw['w_in'], cos, sin, decay, cross, kdec, lw['ret_gn'],
             lw['gm_ln_g'], lw['gm_ln_b'], gmw, gmb, lw['w_out']]

    out_shape = [jax.ShapeDtypeStruct(x2d.shape, _F32),
                 jax.ShapeDtypeStruct((n_streams, RET_HEADS, HEAD_DIM, HEAD_DIM), _F32)]
    out_specs = [row_spec(D_MODEL), state_spec]
    if want_gv:
        out_shape.append(jax.ShapeDtypeStruct((x2d.shape[0], GM_WIDTH), _F32))
        out_specs.append(row_spec(GM_WIDTH))

    scratch = [pltpu.VMEM((R, RET_WIDTH), _BF16),
               pltpu.VMEM((R, RET_WIDTH), _BF16),
               pltpu.VMEM((R, RET_WIDTH), _BF16),
               pltpu.VMEM((R, RET_WIDTH), _F32),
               pltpu.VMEM((R, GM_WIDTH), _F32),
               pltpu.VMEM((R, GM_WIDTH), _BF16),
               pltpu.VMEM((R, D_MODEL), _BF16)]

    kern = functools.partial(_mixer_kernel, nb=nb, T=T, C=C, L=L,
                             has_state=has_state, want_gv=want_gv)
    return pl.pallas_call(
        kern,
        grid=(n_streams // nb, n_t),
        in_specs=in_specs,
        out_specs=out_specs,
        out_shape=out_shape,
        scratch_shapes=scratch,
        compiler_params=pltpu.CompilerParams(
            dimension_semantics=("arbitrary", "arbitrary"),
            vmem_limit_bytes=_VMEM_LIMIT_BYTES),
        name="mixer_state" if has_state else "mixer",
    )(*args)


def _ffn(x2d, lw, g_final, *, R, final):
    n_rows = x2d.shape[0]
    assert n_rows % R == 0
    const = lambda shape: pl.BlockSpec(shape, lambda r: (0,) * len(shape),
                                       pipeline_mode=pl.Buffered(1))
    row_spec = pl.BlockSpec((R, D_MODEL), lambda r: (r, 0))
    return pl.pallas_call(
        functools.partial(_ffn_kernel, final=final),
        grid=(n_rows // R,),
        in_specs=[row_spec, const((1, D_MODEL)), const((D_MODEL, D_FF)),
                  const((D_FF, D_MODEL)), const((1, D_MODEL))],
        out_specs=row_spec,
        out_shape=jax.ShapeDtypeStruct(x2d.shape, _F32),
        scratch_shapes=[pltpu.VMEM((R, D_FF), _BF16)],
        compiler_params=pltpu.CompilerParams(
            dimension_semantics=("arbitrary",),
            vmem_limit_bytes=_VMEM_LIMIT_BYTES),
        name="ffn_final" if final else "ffn",
    )(x2d, lw['g_ffn'], lw['w_up'], lw['w_down'], g_final)


def _pick_tile(S, pref):
    t = min(S, pref)
    while S % t:
        t //= 2
    return t


def _forward(x_prompt, x_sample, state_ret, g_mix, w_in, ret_gn, gm_ln_g, gm_ln_b, gm_w, gm_b,
             w_out, g_ffn, w_up, w_down, g_final):
    B, S, D = x_prompt.shape
    DB, DS, _ = x_sample.shape
    depth = w_in.shape[0]

    T_p = _pick_tile(S, 512)
    C_p = min(T_p, 128)
    L_p = min(S, GM_CHUNK)
    nb_s = math.gcd(DB, 8)
    L_s = min(DS, GM_CHUNK)

    cos_p, sin_p = _rope_tables(jnp.arange(S, dtype=_F32))
    cos_s, sin_s = _rope_tables(PAST_LEN + jnp.arange(DS, dtype=_F32))
    cos_s, sin_s = jnp.tile(cos_s, (nb_s, 1)), jnp.tile(sin_s, (nb_s, 1))
    tab_p = (cos_p, sin_p) + _retention_constants(C_p)
    tab_s = (cos_s, sin_s) + _retention_constants(DS)

    row = lambda a: a.reshape(1, -1)
    g_fin = row(g_final)
    hp = x_prompt.reshape(B * S, D)
    hs = x_sample.reshape(DB * DS, D)
    R_ffn_p = _pick_tile(B * S, 512)
    R_ffn_s = _pick_tile(DB * DS, 512)

    sp, ss, gvs = [], [], []
    for l in range(depth):
        lw = dict(g_mix=row(g_mix[l]), w_in=w_in[l].astype(_BF16), ret_gn=row(ret_gn[l]),
                  gm_ln_g=row(gm_ln_g[l]), gm_ln_b=row(gm_ln_b[l]), gm_w=gm_w[l], gm_b=gm_b[l],
                  w_out=w_out[l].astype(_BF16), g_ffn=row(g_ffn[l]),
                  w_up=w_up[l].astype(_BF16), w_down=w_down[l].astype(_BF16))
        final = l == depth - 1
        hp, s_p = _mixer(hp, None, lw, tab_p, n_streams=B, S=S, nb=1, T=T_p, C=C_p, L=L_p,
                         want_gv=False)
        hp = _ffn(hp, lw, g_fin, R=R_ffn_p, final=final)
        hs, s_s, gv_s = _mixer(hs, state_ret[l], lw, tab_s, n_streams=DB, S=DS, nb=nb_s, T=DS,
                               C=DS, L=L_s, want_gv=True)
        hs = _ffn(hs, lw, g_fin, R=R_ffn_s, final=final)
        sp.append(s_p)
        ss.append(s_s)
        gvs.append(gv_s.reshape(DB, DS, GM_WIDTH))

    return (hp.reshape(B, S, D), hs.reshape(DB, DS, D),
            jnp.stack(sp), jnp.stack(ss), jnp.stack(gvs))


def kernel(x_prompt, x_sample, state_ret, g_mix, w_in, ret_gn, gm_ln_g, gm_ln_b, gm_w, gm_b,
           w_out, g_ffn, w_up, w_down, g_final):
    return _forward(x_prompt, x_sample, state_ret, g_mix, w_in, ret_gn, gm_ln_g, gm_ln_b, gm_w,
                    gm_b, w_out, g_ffn, w_up, w_down, g_final)
```

```python
import functools
import math

import jax
import jax.numpy as jnp
from jax import lax
from jax.experimental import pallas as pl
from jax.experimental.pallas import tpu as pltpu

D_MODEL = 1024
PAST_LEN = 4096
RET_HEADS = 4
HEAD_DIM = 128
RET_WIDTH = RET_HEADS * HEAD_DIM
GM_GROUPS = 4
GM_WIDTH = GM_GROUPS * HEAD_DIM
GM_CHUNK = 128
D_FF = 4 * D_MODEL
ROPE_BASE = 10000.0
EPS = 1e-6
Q_SCALE = HEAD_DIM ** -0.5

_OFF_Q, _OFF_K, _OFF_V, _OFF_GATE, _OFF_U, _OFF_GV = (i * RET_WIDTH for i in range(6))

_VMEM_LIMIT_BYTES = 56 * 1024 * 1024
_FFN_COL_CHUNK = 1024

_BF16 = jnp.bfloat16
_F32 = jnp.float32


def _rms_scale(x, g):
    ms = jnp.mean(x * x, axis=-1, keepdims=True)
    return x * lax.rsqrt(ms + EPS) * g


def _gelu_tanh(x):
    c = math.sqrt(2.0 / math.pi)
    return 0.5 * x * (1.0 + jnp.tanh(c * (x + 0.044715 * (x * x * x))))


def _dot(a, b):
    return jnp.dot(a, b, preferred_element_type=_F32)


def _mixer_kernel(*refs, nb, T, C, L, has_state, want_gv):
    it = iter(refs)
    x_ref = next(it)
    s0_ref = next(it) if has_state else None
    (gmix_ref, win_ref, cos_ref, sin_ref, decay_ref, cross_ref, kdec_ref, retgn_ref,
     lng_ref, lnb_ref, gmw_ref, gmb_ref, wout_ref) = (next(it) for _ in range(13))
    xo_ref = next(it)
    so_ref = next(it)
    gv_ref = next(it) if want_gv else None
    q_s, k_s, v_s, gate_s, u_s, gvb_s, mix_s = (next(it) for _ in range(7))

    t = pl.program_id(1)

    @pl.when(t == 0)
    def _():
        if has_state:
            so_ref[...] = s0_ref[...]
        else:
            so_ref[...] = jnp.zeros_like(so_ref)

    x = x_ref[...]
    h = _rms_scale(x, gmix_ref[...]).astype(_BF16)

    def proj(off):
        return _dot(h, win_ref[:, off:off + RET_WIDTH])

    cos = cos_ref[...]
    sin = sin_ref[...]

    def rope_to(dst, p):
        for hd in range(RET_HEADS):
            sl = slice(hd * HEAD_DIM, (hd + 1) * HEAD_DIM)
            ph = p[:, sl]
            dst[:, sl] = (ph * cos + pltpu.roll(ph, HEAD_DIM // 2, 1) * sin).astype(_BF16)

    pu = proj(_OFF_U)
    pgv = proj(_OFF_GV)
    pgate = proj(_OFF_GATE)
    u_s[...] = _gelu_tanh(pu)
    pv = proj(_OFF_V)
    gvp = _gelu_tanh(pgv)
    mu = jnp.mean(gvp, axis=-1, keepdims=True)
    gc = gvp - mu
    gv = gc * lax.rsqrt(jnp.mean(gc * gc, axis=-1, keepdims=True) + EPS)
    gv = gv * lng_ref[...] + lnb_ref[...]
    if want_gv:
        gv_ref[...] = gv
    gvb_s[...] = gv.astype(_BF16)
    pk = proj(_OFF_K)
    gate_s[...] = pgate * jax.nn.sigmoid(pgate) * retgn_ref[...]
    v_s[...] = pv.astype(_BF16)
    pq = proj(_OFF_Q)
    rope_to(k_s, pk)
    rope_to(q_s, pq)

    row_id = lax.broadcasted_iota(jnp.int32, (L, L), 0)
    col_id = lax.broadcasted_iota(jnp.int32, (L, L), 1)
    causal = row_id >= col_id
    ws = [jnp.where(causal, gmw_ref[g], 0.0).astype(_BF16) for g in range(GM_GROUPS)]

    def retention_block(i, c, hd):
        rows = slice(i * T + c * C, i * T + (c + 1) * C)
        sl = slice(hd * HEAD_DIM, (hd + 1) * HEAD_DIM)
        qh = q_s[rows, sl]
        kh = k_s[rows, sl]
        vh = v_s[rows, sl]
        state = so_ref[i, hd]
        sc = lax.dot_general(qh, kh, (((1,), (1,)), ((), ())), preferred_element_type=_F32)
        sc = (sc * decay_ref[hd]).astype(_BF16)
        o = _dot(sc, vh) + _dot(qh, state.astype(_BF16)) * cross_ref[hd]
        kd = (kh.astype(_F32) * kdec_ref[hd]).astype(_BF16)
        kv = lax.dot_general(kd, vh, (((0,), (0,)), ((), ())), preferred_element_type=_F32)
        so_ref[i, hd] = state * math.exp(_log_gamma(hd) * C) + kv
        on = o * lax.rsqrt(jnp.mean(o * o, axis=-1, keepdims=True) + EPS)
        mix_s[rows, sl] = (gate_s[rows, sl] * on).astype(_BF16)

    def gmlp_block(i, c, g):
        rows = slice(i * T + c * L, i * T + (c + 1) * L)
        sl = slice(g * HEAD_DIM, (g + 1) * HEAD_DIM)
        osl = slice(RET_WIDTH + g * HEAD_DIM, RET_WIDTH + (g + 1) * HEAD_DIM)
        z = _dot(ws[g], gvb_s[rows, sl]) + gmb_ref[g]
        mix_s[rows, osl] = (u_s[rows, sl] * z).astype(_BF16)

    ret_blocks = [(i, c, hd) for i in range(nb) for c in range(T // C) for hd in range(RET_HEADS)]
    gm_blocks = [(i, c, g) for i in range(nb) for c in range(T // L) for g in range(GM_GROUPS)]
    for n in range(max(len(ret_blocks), len(gm_blocks))):
        if n < len(ret_blocks):
            retention_block(*ret_blocks[n])
        if n < len(gm_blocks):
            gmlp_block(*gm_blocks[n])

    y = _dot(mix_s[:, :RET_WIDTH], wout_ref[:RET_WIDTH, :])
    y = y + _dot(mix_s[:, RET_WIDTH:], wout_ref[RET_WIDTH:, :])
    xo_ref[...] = x + y


def _ffn_kernel(x_ref, g_ref, wup_ref, wdn_ref, gfin_ref, o_ref, hid_s, *, final):
    x = x_ref[...]
    h = _rms_scale(x, g_ref[...]).astype(_BF16)
    for j in range(D_FF // _FFN_COL_CHUNK):
        cols = slice(j * _FFN_COL_CHUNK, (j + 1) * _FFN_COL_CHUNK)
        a = jnp.maximum(_dot(h, wup_ref[:, cols]), 0.0)
        hid_s[:, cols] = (a * a).astype(_BF16)
    y = x + _dot(hid_s[...], wdn_ref[...])
    if final:
        y = _rms_scale(y, gfin_ref[...])
    o_ref[...] = y


def _log_gamma(hd):
    return math.log(1.0 - 2.0 ** (-5.0 - hd))


def _retention_constants(C):
    lg = jnp.asarray([_log_gamma(hd) for hd in range(RET_HEADS)], _F32)
    idx = jnp.arange(C, dtype=_F32)
    diff = idx[:, None] - idx[None, :]
    decay = jnp.where(diff >= 0, jnp.exp(lg[:, None, None] * jnp.maximum(diff, 0.0)), 0.0)
    cross = jnp.exp(lg[:, None] * (idx[None, :] + 1.0))
    kdec = jnp.exp(lg[:, None] * (C - 1.0 - idx[None, :]))
    bcast = lambda a: jnp.broadcast_to(a[:, :, None], (RET_HEADS, C, HEAD_DIM))
    return decay * Q_SCALE, bcast(cross * Q_SCALE), bcast(kdec)


def _rope_tables(pos):
    half = HEAD_DIM // 2
    inv_freq = ROPE_BASE ** (-jnp.arange(half, dtype=_F32) / half)
    ang = pos[:, None] * inv_freq[None, :]
    cos, sin = jnp.cos(ang), jnp.sin(ang)
    return jnp.concatenate([cos, cos], axis=-1), jnp.concatenate([-sin, sin], axis=-1)


def _const_spec(shape):
    return pl.BlockSpec(shape, lambda b, t: (0,) * len(shape), pipeline_mode=pl.Buffered(1))


def _mixer(x2d, s0, lw, tables, *, n_streams, S, nb, T, C, L, want_gv):
    has_state = s0 is not None
    assert S % T == 0 and T % C == 0 and T % L == 0 and n_streams % nb == 0
    assert nb == 1 or S == T
    R = nb * T
    n_t = S // T
    cos, sin, decay, cross, kdec = tables
    gmw = lw['gm_w'][:, :L, :L]
    gmb = jnp.broadcast_to(lw['gm_b'][:, :L, None], (GM_GROUPS, L, HEAD_DIM))

    row_spec = lambda width: pl.BlockSpec((R, width), lambda b, t: (b * n_t + t, 0))
    state_spec = pl.BlockSpec((nb, RET_HEADS, HEAD_DIM, HEAD_DIM), lambda b, t: (b, 0, 0, 0))

    in_specs = [row_spec(D_MODEL)]
    args = [x2d]
    if has_state:
        in_specs.append(state_spec)
        args.append(s0)
    in_specs += [
        _const_spec((1, D_MODEL)),
        _const_spec((D_MODEL, 6 * RET_WIDTH)),
        pl.BlockSpec((R, HEAD_DIM), lambda b, t: (t, 0)),
        pl.BlockSpec((R, HEAD_DIM), lambda b, t: (t, 0)),
        _const_spec((RET_HEADS, C, C)),
        _const_spec((RET_HEADS, C, HEAD_DIM)),
        _const_spec((RET_HEADS, C, HEAD_DIM)),
        _const_spec((1, RET_WIDTH)),
        _const_spec((1, GM_WIDTH)),
        _const_spec((1, GM_WIDTH)),
        _const_spec((GM_GROUPS, L, L)),
        _const_spec((GM_GROUPS, L, HEAD_DIM)),
        _const_spec((D_MODEL, D_MODEL)),
    ]
    args += [lw['g_mix'], lw['w_in'], cos, sin, decay, cross, kdec, lw['ret_gn'],
             lw['gm_ln_g'], lw['gm_ln_b'], gmw, gmb, lw['w_out']]

    out_shape = [jax.ShapeDtypeStruct(x2d.shape, _F32),
                 jax.ShapeDtypeStruct((n_streams, RET_HEADS, HEAD_DIM, HEAD_DIM), _F32)]
    out_specs = [row_spec(D_MODEL), state_spec]
    if want_gv:
        out_shape.append(jax.ShapeDtypeStruct((x2d.shape[0], GM_WIDTH), _F32))
        out_specs.append(row_spec(GM_WIDTH))

    scratch = [pltpu.VMEM((R, RET_WIDTH), _BF16),
               pltpu.VMEM((R, RET_WIDTH), _BF16),
               pltpu.VMEM((R, RET_WIDTH), _BF16),
               pltpu.VMEM((R, RET_WIDTH), _F32),
               pltpu.VMEM((R, GM_WIDTH), _F32),
               pltpu.VMEM((R, GM_WIDTH), _BF16),
               pltpu.VMEM((R, D_MODEL), _BF16)]

    kern = functools.partial(_mixer_kernel, nb=nb, T=T, C=C, L=L,
                             has_state=has_state, want_gv=want_gv)
    return pl.pallas_call(
        kern,
        grid=(n_streams // nb, n_t),
        in_specs=in_specs,
        out_specs=out_specs,
        out_shape=out_shape,
        scratch_shapes=scratch,
        compiler_params=pltpu.CompilerParams(
            dimension_semantics=("arbitrary", "arbitrary"),
            vmem_limit_bytes=_VMEM_LIMIT_BYTES),
        name="mixer_state" if has_state else "mixer",
    )(*args)


def _ffn(x2d, lw, g_final, *, R, final):
    n_rows = x2d.shape[0]
    assert n_rows % R == 0
    const = lambda shape: pl.BlockSpec(shape, lambda r: (0,) * len(shape),
                                       pipeline_mode=pl.Buffered(1))
    row_spec = pl.BlockSpec((R, D_MODEL), lambda r: (r, 0))
    return pl.pallas_call(
        functools.partial(_ffn_kernel, final=final),
        grid=(n_rows // R,),
        in_specs=[row_spec, const((1, D_MODEL)), const((D_MODEL, D_FF)),
                  const((D_FF, D_MODEL)), const((1, D_MODEL))],
        out_specs=row_spec,
        out_shape=jax.ShapeDtypeStruct(x2d.shape, _F32),
        scratch_shapes=[pltpu.VMEM((R, D_FF), _BF16)],
        compiler_params=pltpu.CompilerParams(
            dimension_semantics=("arbitrary",),
            vmem_limit_bytes=_VMEM_LIMIT_BYTES),
        name="ffn_final" if final else "ffn",
    )(x2d, lw['g_ffn'], lw['w_up'], lw['w_down'], g_final)


def _pick_tile(S, pref):
    t = min(S, pref)
    while S % t:
        t //= 2
    return t


def _forward(x_prompt, x_sample, state_ret, g_mix, w_in, ret_gn, gm_ln_g, gm_ln_b, gm_w, gm_b,
             w_out, g_ffn, w_up, w_down, g_final):
    B, S, D = x_prompt.shape
    DB, DS, _ = x_sample.shape
    depth = w_in.shape[0]

    T_p = _pick_tile(S, 512)
    C_p = min(T_p, 128)
    L_p = min(S, GM_CHUNK)
    nb_s = math.gcd(DB, 8)
    L_s = min(DS, GM_CHUNK)

    cos_p, sin_p = _rope_tables(jnp.arange(S, dtype=_F32))
    cos_s, sin_s = _rope_tables(PAST_LEN + jnp.arange(DS, dtype=_F32))
    cos_s, sin_s = jnp.tile(cos_s, (nb_s, 1)), jnp.tile(sin_s, (nb_s, 1))
    tab_p = (cos_p, sin_p) + _retention_constants(C_p)
    tab_s = (cos_s, sin_s) + _retention_constants(DS)

    row = lambda a: a.reshape(1, -1)
    g_fin = row(g_final)
    hp = x_prompt.reshape(B * S, D)
    hs = x_sample.reshape(DB * DS, D)
    R_ffn_p = _pick_tile(B * S, 512)
    R_ffn_s = _pick_tile(DB * DS, 512)

    sp, ss, gvs = [], [], []
    for l in range(depth):
        lw = dict(g_mix=row(g_mix[l]), w_in=w_in[l].astype(_BF16), ret_gn=row(ret_gn[l]),
                  gm_ln_g=row(gm_ln_g[l]), gm_ln_b=row(gm_ln_b[l]), gm_w=gm_w[l], gm_b=gm_b[l],
                  w_out=w_out[l].astype(_BF16), g_ffn=row(g_ffn[l]),
                  w_up=w_up[l].astype(_BF16), w_down=w_down[l].astype(_BF16))
        final = l == depth - 1
        hp, s_p = _mixer(hp, None, lw, tab_p, n_streams=B, S=S, nb=1, T=T_p, C=C_p, L=L_p,
                         want_gv=False)
        hp = _ffn(hp, lw, g_fin, R=R_ffn_p, final=final)
        hs, s_s, gv_s = _mixer(hs, state_ret[l], lw, tab_s, n_streams=DB, S=DS, nb=nb_s, T=DS,
                               C=DS, L=L_s, want_gv=True)
        hs = _ffn(hs, lw, g_fin, R=R_ffn_s, final=final)
        sp.append(s_p)
        ss.append(s_s)
        gvs.append(gv_s.reshape(DB, DS, GM_WIDTH))

    return (hp.reshape(B, S, D), hs.reshape(DB, DS, D),
            jnp.stack(sp), jnp.stack(ss), jnp.stack(gvs))


def kernel(x_prompt, x_sample, state_ret, g_mix, w_in, ret_gn, gm_ln_g, gm_ln_b, gm_w, gm_b,
           w_out, g_ffn, w_up, w_down, g_final):
    return _forward(x_prompt, x_sample, state_ret, g_mix, w_in, ret_gn, gm_ln_g, gm_ln_b, gm_w,
                    gm_b, w_out, g_ffn, w_up, w_down, g_final)
```

```python
import functools
import math

import jax
import jax.numpy as jnp
from jax import lax
from jax.experimental import pallas as pl
from jax.experimental.pallas import tpu as pltpu

D_MODEL = 1024
PAST_LEN = 4096
RET_HEADS = 4
HEAD_DIM = 128
RET_WIDTH = RET_HEADS * HEAD_DIM
GM_GROUPS = 4
GM_WIDTH = GM_GROUPS * HEAD_DIM
GM_CHUNK = 128
D_FF = 4 * D_MODEL
ROPE_BASE = 10000.0
EPS = 1e-6
Q_SCALE = HEAD_DIM ** -0.5

_OFF_Q, _OFF_K, _OFF_V, _OFF_GATE, _OFF_U, _OFF_GV = (i * RET_WIDTH for i in range(6))

_VMEM_LIMIT_BYTES = 56 * 1024 * 1024
_FFN_COL_CHUNK = 1024
_FFN_TILE_ROWS = 1024
_FFN_GROUP_ROWS = 512
_PROMPT_TILE_ROWS = 1024
_PIPELINE_GROUP_ROWS = 256
_SAMPLE_STREAMS_PER_TILE = 8

_BF16 = jnp.bfloat16
_F32 = jnp.float32


def _rms_scale(x, g):
    ms = jnp.mean(x * x, axis=-1, keepdims=True)
    return x * lax.rsqrt(ms + EPS) * g


def _gelu_tanh(x):
    c = math.sqrt(2.0 / math.pi)
    return 0.5 * x * (1.0 + jnp.tanh(c * (x + 0.044715 * (x * x * x))))


def _dot(a, b):
    return jnp.dot(a, b, preferred_element_type=_F32)


def _interleave(first, second):
    if not second:
        return list(first)
    if not first:
        return list(second)
    out, j = [], 0
    for i, task in enumerate(first):
        out.append(task)
        want = ((i + 1) * len(second)) // len(first)
        while j < want:
            out.append(second[j])
            j += 1
    return out


def _mixer_kernel(*refs, nb, T, C, n_sub, has_state, want_gv):
    it = iter(refs)
    x_ref = next(it)
    s0_ref = next(it) if has_state else None
    (gmix_ref, win_ref, cos_ref, sin_ref, decay_ref, cross_ref, kdec_ref, retgn_ref,
     lng_ref, lnb_ref, gmw_ref, gmb_ref, wout_ref) = (next(it) for _ in range(13))
    xo_ref = next(it)
    so_ref = next(it)
    gv_ref = next(it) if want_gv else None
    q_s, k_s, v_s, gate_s, u_s, gvb_s, mix_s = (next(it) for _ in range(7))

    R = nb * T
    M = R // n_sub
    assert M % C == 0
    t = pl.program_id(1)

    @pl.when(t == 0)
    def _():
        if has_state:
            so_ref[...] = s0_ref[...]
        else:
            so_ref[...] = jnp.zeros_like(so_ref)

    row_id = lax.broadcasted_iota(jnp.int32, (C, C), 0)
    col_id = lax.broadcasted_iota(jnp.int32, (C, C), 1)
    causal = row_id >= col_id
    ws = [jnp.where(causal, gmw_ref[g], 0.0).astype(_BF16) for g in range(GM_GROUPS)]

    def projection_tasks(j):
        rows = slice(j * M, (j + 1) * M)
        val = {}

        def norm():
            val['h'] = _rms_scale(x_ref[rows, :], gmix_ref[...]).astype(_BF16)

        def proj(name, off):
            def task():
                val[name] = _dot(val['h'], win_ref[:, off:off + RET_WIDTH])
            return task

        def rope_to(dst, name):
            def task():
                cos = cos_ref[rows, :]
                sin = sin_ref[rows, :]
                p = val.pop(name)
                for hd in range(RET_HEADS):
                    sl = slice(hd * HEAD_DIM, (hd + 1) * HEAD_DIM)
                    ph = p[:, sl]
                    dst[rows, sl] = (ph * cos + pltpu.roll(ph, HEAD_DIM // 2, 1) * sin).astype(_BF16)
            return task

        def u_act():
            u_s[rows, :] = _gelu_tanh(val.pop('u'))

        def gv_act():
            gvp = _gelu_tanh(val.pop('gv'))
            mu = jnp.mean(gvp, axis=-1, keepdims=True)
            gc = gvp - mu
            gv = gc * lax.rsqrt(jnp.mean(gc * gc, axis=-1, keepdims=True) + EPS)
            gv = gv * lng_ref[...] + lnb_ref[...]
            if want_gv:
                gv_ref[rows, :] = gv
            gvb_s[rows, :] = gv.astype(_BF16)

        def gate_v_act():
            pgate = val.pop('gate')
            gate_s[rows, :] = pgate * jax.nn.sigmoid(pgate) * retgn_ref[...]
            v_s[rows, :] = val.pop('v').astype(_BF16)

        return [norm, proj('k', _OFF_K), proj('q', _OFF_Q), proj('v', _OFF_V), rope_to(k_s, 'k'),
                proj('gate', _OFF_GATE), rope_to(q_s, 'q'), proj('u', _OFF_U), gate_v_act,
                proj('gv', _OFF_GV), u_act, gv_act]

    def retention_scores(stream, r0, hd, val):
        rows = slice(r0, r0 + C)
        sl = slice(hd * HEAD_DIM, (hd + 1) * HEAD_DIM)
        qh = q_s[rows, sl]
        kh = k_s[rows, sl]
        state = so_ref[stream, hd]
        val['sc'] = lax.dot_general(qh, kh, (((1,), (1,)), ((), ())),
                                    preferred_element_type=_F32)
        val['cross'] = _dot(qh, state.astype(_BF16))
        kd = (kh.astype(_F32) * kdec_ref[hd]).astype(_BF16)
        kv = lax.dot_general(kd, v_s[rows, sl], (((0,), (0,)), ((), ())),
                             preferred_element_type=_F32)
        so_ref[stream, hd] = state * math.exp(_log_gamma(hd) * C) + kv

    def retention_output(r0, hd, val):
        rows = slice(r0, r0 + C)
        sl = slice(hd * HEAD_DIM, (hd + 1) * HEAD_DIM)
        sc = (val.pop('sc') * decay_ref[hd]).astype(_BF16)
        o = _dot(sc, v_s[rows, sl]) + val.pop('cross') * cross_ref[hd]
        on = o * lax.rsqrt(jnp.mean(o * o, axis=-1, keepdims=True) + EPS)
        mix_s[rows, sl] = (gate_s[rows, sl] * on).astype(_BF16)

    def gmlp_block(r0, g):
        rows = slice(r0, r0 + C)
        sl = slice(g * HEAD_DIM, (g + 1) * HEAD_DIM)
        osl = slice(RET_WIDTH + g * HEAD_DIM, RET_WIDTH + (g + 1) * HEAD_DIM)
        z = _dot(ws[g], gvb_s[rows, sl]) + gmb_ref[g]
        mix_s[rows, osl] = (u_s[rows, sl] * z).astype(_BF16)

    def chunk_tasks(j):
        items = [(r0, hd) for r0 in range(j * M, (j + 1) * M, C) for hd in range(RET_HEADS)]
        vals = [{} for _ in items]
        tasks = []
        for n in range(len(items) + 1):
            if n < len(items):
                r0, hd = items[n]
                tasks.append(functools.partial(retention_scores, r0 // T, r0, hd, vals[n]))
            if n >= 1:
                r0, hd = items[n - 1]
                tasks.append(functools.partial(gmlp_block, r0, hd))
                tasks.append(functools.partial(retention_output, r0, hd, vals[n - 1]))
        return tasks

    def output_tasks(j):
        rows = slice(j * M, (j + 1) * M)
        val = {}

        def first():
            val['y'] = _dot(mix_s[rows, :RET_WIDTH], wout_ref[:RET_WIDTH, :])

        def second():
            y = val.pop('y') + _dot(mix_s[rows, RET_WIDTH:], wout_ref[RET_WIDTH:, :])
            xo_ref[rows, :] = x_ref[rows, :] + y

        return [first, second]

    program = []
    for step in range(n_sub + 2):
        proj_t = projection_tasks(step) if step < n_sub else []
        chunk_t = chunk_tasks(step - 1) if 0 <= step - 1 < n_sub else []
        out_t = output_tasks(step - 2) if 0 <= step - 2 < n_sub else []
        program += _interleave(_interleave(proj_t, chunk_t) if proj_t else chunk_t, out_t)
    for task in program:
        task()


def _ffn_kernel(x_ref, g_ref, wup_ref, wdn_ref, gfin_ref, o_ref, hid_s, *, n_sub, final):
    M = x_ref.shape[0] // n_sub
    n_col = D_FF // _FFN_COL_CHUNK

    def group_tasks(j):
        rows = slice(j * M, (j + 1) * M)
        val = {}

        def norm():
            val['h'] = _rms_scale(x_ref[rows, :], g_ref[...]).astype(_BF16)

        def up(n):
            def task():
                cols = slice(n * _FFN_COL_CHUNK, (n + 1) * _FFN_COL_CHUNK)
                a = jnp.maximum(_dot(val['h'], wup_ref[:, cols]), 0.0)
                hid_s[rows, cols] = (a * a).astype(_BF16)
            return task

        def down():
            y = x_ref[rows, :] + _dot(hid_s[rows, :], wdn_ref[...])
            if final:
                y = _rms_scale(y, gfin_ref[...])
            o_ref[rows, :] = y

        return norm, [up(n) for n in range(n_col)] + [down]

    groups = [group_tasks(j) for j in range(n_sub)]
    groups[0][0]()
    for j, (_, matmuls) in enumerate(groups):
        for n, task in enumerate(matmuls):
            task()
            if n == 0 and j + 1 < n_sub:
                groups[j + 1][0]()


def _log_gamma(hd):
    return math.log(1.0 - 2.0 ** (-5.0 - hd))


def _retention_constants(C):
    lg = jnp.asarray([_log_gamma(hd) for hd in range(RET_HEADS)], _F32)
    idx = jnp.arange(C, dtype=_F32)
    diff = idx[:, None] - idx[None, :]
    decay = jnp.where(diff >= 0, jnp.exp(lg[:, None, None] * jnp.maximum(diff, 0.0)), 0.0)
    cross = jnp.exp(lg[:, None] * (idx[None, :] + 1.0))
    kdec = jnp.exp(lg[:, None] * (C - 1.0 - idx[None, :]))
    bcast = lambda a: jnp.broadcast_to(a[:, :, None], (RET_HEADS, C, HEAD_DIM))
    return decay * Q_SCALE, bcast(cross * Q_SCALE), bcast(kdec)


def _rope_tables(pos):
    half = HEAD_DIM // 2
    inv_freq = ROPE_BASE ** (-jnp.arange(half, dtype=_F32) / half)
    ang = pos[:, None] * inv_freq[None, :]
    cos, sin = jnp.cos(ang), jnp.sin(ang)
    return jnp.concatenate([cos, cos], axis=-1), jnp.concatenate([-sin, sin], axis=-1)


def _const_spec(shape):
    return pl.BlockSpec(shape, lambda b, t: (0,) * len(shape), pipeline_mode=pl.Buffered(1))


def _layer_spec(shape, layer, n_grid):
    index = (lambda b, t: (layer, 0, 0)) if n_grid == 2 else (lambda r: (layer, 0, 0))
    return pl.BlockSpec((None,) + shape, index, pipeline_mode=pl.Buffered(1))


def _mixer(x2d, s0, lw, tables, *, layer, n_streams, S, nb, T, C, n_sub, want_gv):
    has_state = s0 is not None
    assert S % T == 0 and T % C == 0 and n_streams % nb == 0
    assert C == min(S, GM_CHUNK)
    assert nb == 1 or S == T
    R = nb * T
    n_t = S // T
    cos, sin, decay, cross, kdec = tables
    gmw = lw['gm_w'][:, :C, :C]
    gmb = jnp.broadcast_to(lw['gm_b'][:, :C, None], (GM_GROUPS, C, HEAD_DIM))

    row_spec = lambda width: pl.BlockSpec((R, width), lambda b, t: (b * n_t + t, 0))
    state_spec = pl.BlockSpec((nb, RET_HEADS, HEAD_DIM, HEAD_DIM), lambda b, t: (b, 0, 0, 0))

    in_specs = [row_spec(D_MODEL)]
    args = [x2d]
    if has_state:
        in_specs.append(state_spec)
        args.append(s0)
    in_specs += [
        _const_spec((1, D_MODEL)),
        _layer_spec((D_MODEL, 6 * RET_WIDTH), layer, 2),
        pl.BlockSpec((R, HEAD_DIM), lambda b, t: (t, 0)),
        pl.BlockSpec((R, HEAD_DIM), lambda b, t: (t, 0)),
        _const_spec((RET_HEADS, C, C)),
        _const_spec((RET_HEADS, C, HEAD_DIM)),
        _const_spec((RET_HEADS, C, HEAD_DIM)),
        _const_spec((1, RET_WIDTH)),
        _const_spec((1, GM_WIDTH)),
        _const_spec((1, GM_WIDTH)),
        _const_spec((GM_GROUPS, C, C)),
        _const_spec((GM_GROUPS, C, HEAD_DIM)),
        _layer_spec((D_MODEL, D_MODEL), layer, 2),
    ]
    args += [lw['g_mix'], lw['w_in'], cos, sin, decay, cross, kdec, lw['ret_gn'],
             lw['gm_ln_g'], lw['gm_ln_b'], gmw, gmb, lw['w_out']]

    out_shape = [jax.ShapeDtypeStruct(x2d.shape, _F32),
                 jax.ShapeDtypeStruct((n_streams, RET_HEADS, HEAD_DIM, HEAD_DIM), _F32)]
    out_specs = [row_spec(D_MODEL), state_spec]
    if want_gv:
        out_shape.append(jax.ShapeDtypeStruct((x2d.shape[0], GM_WIDTH), _F32))
        out_specs.append(row_spec(GM_WIDTH))

    scratch = [pltpu.VMEM((R, RET_WIDTH), _BF16),
               pltpu.VMEM((R, RET_WIDTH), _BF16),
               pltpu.VMEM((R, RET_WIDTH), _BF16),
               pltpu.VMEM((R, RET_WIDTH), _F32),
               pltpu.VMEM((R, GM_WIDTH), _F32),
               pltpu.VMEM((R, GM_WIDTH), _BF16),
               pltpu.VMEM((R, D_MODEL), _BF16)]

    kern = functools.partial(_mixer_kernel, nb=nb, T=T, C=C, n_sub=n_sub,
                             has_state=has_state, want_gv=want_gv)
    return pl.pallas_call(
        kern,
        grid=(n_streams // nb, n_t),
        in_specs=in_specs,
        out_specs=out_specs,
        out_shape=out_shape,
        scratch_shapes=scratch,
        compiler_params=pltpu.CompilerParams(
            dimension_semantics=("arbitrary", "arbitrary"),
            vmem_limit_bytes=_VMEM_LIMIT_BYTES),
        name="mixer_state" if has_state else "mixer",
    )(*args)


def _ffn(x2d, lw, g_final, *, layer, R, final):
    n_rows = x2d.shape[0]
    assert n_rows % R == 0
    n_sub = max(1, R // _FFN_GROUP_ROWS)
    const = lambda shape: pl.BlockSpec(shape, lambda r: (0,) * len(shape),
                                       pipeline_mode=pl.Buffered(1))
    row_spec = pl.BlockSpec((R, D_MODEL), lambda r: (r, 0))
    return pl.pallas_call(
        functools.partial(_ffn_kernel, n_sub=n_sub, final=final),
        grid=(n_rows // R,),
        in_specs=[row_spec, const((1, D_MODEL)), _layer_spec((D_MODEL, D_FF), layer, 1),
                  _layer_spec((D_FF, D_MODEL), layer, 1), const((1, D_MODEL))],
        out_specs=row_spec,
        out_shape=jax.ShapeDtypeStruct(x2d.shape, _F32),
        scratch_shapes=[pltpu.VMEM((R, D_FF), _BF16)],
        compiler_params=pltpu.CompilerParams(
            dimension_semantics=("arbitrary",),
            vmem_limit_bytes=_VMEM_LIMIT_BYTES),
        name="ffn_final" if final else "ffn",
    )(x2d, lw['g_ffn'], lw['w_up'], lw['w_down'], g_final)


def _pick_tile(S, pref):
    t = min(S, pref)
    while S % t:
        t //= 2
    return t


def _forward(x_prompt, x_sample, state_ret, g_mix, w_in, ret_gn, gm_ln_g, gm_ln_b, gm_w, gm_b,
             w_out, g_ffn, w_up, w_down, g_final):
    B, S, D = x_prompt.shape
    DB, DS, _ = x_sample.shape
    depth = w_in.shape[0]

    T_p = _pick_tile(S, _PROMPT_TILE_ROWS)
    C_p = min(S, GM_CHUNK)
    sub_p = max(1, T_p // _PIPELINE_GROUP_ROWS)
    nb_s = math.gcd(DB, _SAMPLE_STREAMS_PER_TILE)

    cos_p, sin_p = _rope_tables(jnp.arange(S, dtype=_F32))
    cos_s, sin_s = _rope_tables(PAST_LEN + jnp.arange(DS, dtype=_F32))
    cos_s, sin_s = jnp.tile(cos_s, (nb_s, 1)), jnp.tile(sin_s, (nb_s, 1))
    tab_p = (cos_p, sin_p) + _retention_constants(C_p)
    tab_s = (cos_s, sin_s) + _retention_constants(DS)

    row = lambda a: a.reshape(1, -1)
    g_fin = row(g_final)
    hp = x_prompt.reshape(B * S, D)
    hs = x_sample.reshape(DB * DS, D)
    R_ffn_p = _pick_tile(B * S, _FFN_TILE_ROWS)
    R_ffn_s = _pick_tile(DB * DS, _FFN_TILE_ROWS)

    w_in_b, w_out_b = w_in.astype(_BF16), w_out.astype(_BF16)
    w_up_b, w_down_b = w_up.astype(_BF16), w_down.astype(_BF16)

    sp, ss, gvs = [], [], []
    for l in range(depth):
        lw = dict(g_mix=row(g_mix[l]), w_in=w_in_b, ret_gn=row(ret_gn[l]),
                  gm_ln_g=row(gm_ln_g[l]), gm_ln_b=row(gm_ln_b[l]), gm_w=gm_w[l], gm_b=gm_b[l],
                  w_out=w_out_b, g_ffn=row(g_ffn[l]), w_up=w_up_b, w_down=w_down_b)
        final = l == depth - 1
        hp, s_p = _mixer(hp, None, lw, tab_p, layer=l, n_streams=B, S=S, nb=1, T=T_p, C=C_p,
                         n_sub=sub_p, want_gv=False)
        hp = _ffn(hp, lw, g_fin, layer=l, R=R_ffn_p, final=final)
        hs, s_s, gv_s = _mixer(hs, state_ret[l], lw, tab_s, layer=l, n_streams=DB, S=DS, nb=nb_s,
                               T=DS, C=DS, n_sub=1, want_gv=True)
        hs = _ffn(hs, lw, g_fin, layer=l, R=R_ffn_s, final=final)
        sp.append(s_p)
        ss.append(s_s)
        gvs.append(gv_s.reshape(DB, DS, GM_WIDTH))

    return (hp.reshape(B, S, D), hs.reshape(DB, DS, D),
            jnp.stack(sp), jnp.stack(ss), jnp.stack(gvs))


def kernel(x_prompt, x_sample, state_ret, g_mix, w_in, ret_gn, gm_ln_g, gm_ln_b, gm_w, gm_b,
           w_out, g_ffn, w_up, w_down, g_final):
    return _forward(x_prompt, x_sample, state_ret, g_mix, w_in, ret_gn, gm_ln_g, gm_ln_b, gm_w,
                    gm_b, w_out, g_ffn, w_up, w_down, g_final)
```

```python
import functools
import math

import jax
import jax.numpy as jnp
from jax import lax
from jax.experimental import pallas as pl
from jax.experimental.pallas import tpu as pltpu

D_MODEL = 1024
PAST_LEN = 4096
RET_HEADS = 4
HEAD_DIM = 128
RET_WIDTH = RET_HEADS * HEAD_DIM
GM_GROUPS = 4
GM_WIDTH = GM_GROUPS * HEAD_DIM
GM_CHUNK = 128
D_FF = 4 * D_MODEL
ROPE_BASE = 10000.0
EPS = 1e-6
Q_SCALE = HEAD_DIM ** -0.5

_OFF_Q, _OFF_K, _OFF_V, _OFF_GATE, _OFF_U, _OFF_GV = (i * RET_WIDTH for i in range(6))

_VMEM_LIMIT_BYTES = 56 * 1024 * 1024
_FFN_COL_CHUNK = 1024
_FFN_TILE_ROWS = 1024
_FFN_GROUP_ROWS = 512
_PROMPT_TILE_ROWS = 1024
_PIPELINE_GROUP_ROWS = 512
_SAMPLE_STREAMS_PER_TILE = 8

_BF16 = jnp.bfloat16
_F32 = jnp.float32


def _rms_scale(x, g):
    ms = jnp.mean(x * x, axis=-1, keepdims=True)
    return x * lax.rsqrt(ms + EPS) * g


def _gelu_tanh(x):
    c = math.sqrt(2.0 / math.pi)
    return 0.5 * x * (1.0 + jnp.tanh(c * (x + 0.044715 * (x * x * x))))


def _dot(a, b):
    return jnp.dot(a, b, preferred_element_type=_F32)


def _interleave(first, second):
    if not second:
        return list(first)
    if not first:
        return list(second)
    out, j = [], 0
    for i, task in enumerate(first):
        out.append(task)
        want = ((i + 1) * len(second)) // len(first)
        while j < want:
            out.append(second[j])
            j += 1
    return out


def _mixer_kernel(*refs, nb, T, C, n_sub, has_state, want_gv):
    it = iter(refs)
    x_ref = next(it)
    s0_ref = next(it) if has_state else None
    (gmix_ref, win_ref, cos_ref, sin_ref, decay_ref, cross_ref, kdec_ref, retgn_ref,
     lng_ref, lnb_ref, gmw_ref, gmb_ref, wout_ref) = (next(it) for _ in range(13))
    xo_ref = next(it)
    so_ref = next(it)
    gv_ref = next(it) if want_gv else None
    q_s, k_s, v_s, gate_s, u_s, gvb_s, mix_s = (next(it) for _ in range(7))

    R = nb * T
    M = R // n_sub
    assert M % C == 0
    t = pl.program_id(1)

    @pl.when(t == 0)
    def _():
        if has_state:
            so_ref[...] = s0_ref[...]
        else:
            so_ref[...] = jnp.zeros_like(so_ref)

    row_id = lax.broadcasted_iota(jnp.int32, (C, C), 0)
    col_id = lax.broadcasted_iota(jnp.int32, (C, C), 1)
    causal = row_id >= col_id
    ws = [jnp.where(causal, gmw_ref[g], 0.0).astype(_BF16) for g in range(GM_GROUPS)]

    def projection_tasks(j):
        rows = slice(j * M, (j + 1) * M)
        val = {}

        def norm():
            val['h'] = _rms_scale(x_ref[rows, :], gmix_ref[...]).astype(_BF16)

        def proj(first, second, off):
            def task():
                p = _dot(val['h'], win_ref[:, off:off + 2 * RET_WIDTH])
                val[first], val[second] = p[:, :RET_WIDTH], p[:, RET_WIDTH:]
            return task

        def rope_to(dst, name):
            def task():
                cos = cos_ref[rows, :]
                sin = sin_ref[rows, :]
                p = val.pop(name)
                for hd in range(RET_HEADS):
                    sl = slice(hd * HEAD_DIM, (hd + 1) * HEAD_DIM)
                    ph = p[:, sl]
                    dst[rows, sl] = (ph * cos + pltpu.roll(ph, HEAD_DIM // 2, 1) * sin).astype(_BF16)
            return task

        def u_act():
            u_s[rows, :] = _gelu_tanh(val.pop('u'))

        def gv_act():
            gvp = _gelu_tanh(val.pop('gv'))
            mu = jnp.mean(gvp, axis=-1, keepdims=True)
            gc = gvp - mu
            gv = gc * lax.rsqrt(jnp.mean(gc * gc, axis=-1, keepdims=True) + EPS)
            gv = gv * lng_ref[...] + lnb_ref[...]
            if want_gv:
                gv_ref[rows, :] = gv
            gvb_s[rows, :] = gv.astype(_BF16)

        def gate_v_act():
            pgate = val.pop('gate')
            gate_s[rows, :] = pgate * jax.nn.sigmoid(pgate) * retgn_ref[...]
            v_s[rows, :] = val.pop('v').astype(_BF16)

        return [norm, proj('q', 'k', _OFF_Q), proj('v', 'gate', _OFF_V), rope_to(k_s, 'k'),
                rope_to(q_s, 'q'), proj('u', 'gv', _OFF_U), gate_v_act, u_act, gv_act]

    def retention_scores(stream, r0, hd, val):
        rows = slice(r0, r0 + C)
        sl = slice(hd * HEAD_DIM, (hd + 1) * HEAD_DIM)
        qh = q_s[rows, sl]
        kh = k_s[rows, sl]
        state = so_ref[stream, hd]
        val['sc'] = lax.dot_general(qh, kh, (((1,), (1,)), ((), ())),
                                    preferred_element_type=_F32)
        val['cross'] = _dot(qh, state.astype(_BF16))
        kd = (kh.astype(_F32) * kdec_ref[hd]).astype(_BF16)
        kv = lax.dot_general(kd, v_s[rows, sl], (((0,), (0,)), ((), ())),
                             preferred_element_type=_F32)
        so_ref[stream, hd] = state * math.exp(_log_gamma(hd) * C) + kv

    def retention_output(r0, hd, val):
        rows = slice(r0, r0 + C)
        sl = slice(hd * HEAD_DIM, (hd + 1) * HEAD_DIM)
        sc = (val.pop('sc') * decay_ref[hd]).astype(_BF16)
        o = _dot(sc, v_s[rows, sl]) + val.pop('cross') * cross_ref[hd]
        on = o * lax.rsqrt(jnp.mean(o * o, axis=-1, keepdims=True) + EPS)
        mix_s[rows, sl] = (gate_s[rows, sl] * on).astype(_BF16)

    def gmlp_block(r0, g):
        rows = slice(r0, r0 + C)
        sl = slice(g * HEAD_DIM, (g + 1) * HEAD_DIM)
        osl = slice(RET_WIDTH + g * HEAD_DIM, RET_WIDTH + (g + 1) * HEAD_DIM)
        z = _dot(ws[g], gvb_s[rows, sl]) + gmb_ref[g]
        mix_s[rows, osl] = (u_s[rows, sl] * z).astype(_BF16)

    def chunk_tasks(j):
        items = [(r0, hd) for r0 in range(j * M, (j + 1) * M, C) for hd in range(RET_HEADS)]
        vals = [{} for _ in items]
        tasks = []
        for n in range(len(items) + 1):
            if n < len(items):
                r0, hd = items[n]
                tasks.append(functools.partial(retention_scores, r0 // T, r0, hd, vals[n]))
            if n >= 1:
                r0, hd = items[n - 1]
                tasks.append(functools.partial(gmlp_block, r0, hd))
                tasks.append(functools.partial(retention_output, r0, hd, vals[n - 1]))
        return tasks

    def output_tasks(j):
        rows = slice(j * M, (j + 1) * M)
        val = {}

        def first():
            val['y'] = _dot(mix_s[rows, :RET_WIDTH], wout_ref[:RET_WIDTH, :])

        def second():
            y = val.pop('y') + _dot(mix_s[rows, RET_WIDTH:], wout_ref[RET_WIDTH:, :])
            xo_ref[rows, :] = x_ref[rows, :] + y

        return [first, second]

    program = []
    for step in range(n_sub + 2):
        proj_t = projection_tasks(step) if step < n_sub else []
        chunk_t = chunk_tasks(step - 1) if 0 <= step - 1 < n_sub else []
        out_t = output_tasks(step - 2) if 0 <= step - 2 < n_sub else []
        program += _interleave(_interleave(proj_t, chunk_t) if proj_t else chunk_t, out_t)
    for task in program:
        task()


def _ffn_kernel(x_ref, g_ref, wup_ref, wdn_ref, gfin_ref, o_ref, hid_s, *, n_sub, final):
    M = x_ref.shape[0] // n_sub
    n_col = D_FF // _FFN_COL_CHUNK

    def group_tasks(j):
        rows = slice(j * M, (j + 1) * M)
        val = {}

        def norm():
            val['h'] = _rms_scale(x_ref[rows, :], g_ref[...]).astype(_BF16)

        def up(n):
            def task():
                cols = slice(n * _FFN_COL_CHUNK, (n + 1) * _FFN_COL_CHUNK)
                a = jnp.maximum(_dot(val['h'], wup_ref[:, cols]), 0.0)
                hid_s[rows, cols] = (a * a).astype(_BF16)
            return task

        def down():
            y = x_ref[rows, :] + _dot(hid_s[rows, :], wdn_ref[...])
            if final:
                y = _rms_scale(y, gfin_ref[...])
            o_ref[rows, :] = y

        return norm, [up(n) for n in range(n_col)] + [down]

    groups = [group_tasks(j) for j in range(n_sub)]
    groups[0][0]()
    for j, (_, matmuls) in enumerate(groups):
        for n, task in enumerate(matmuls):
            task()
            if n == 0 and j + 1 < n_sub:
                groups[j + 1][0]()


def _log_gamma(hd):
    return math.log(1.0 - 2.0 ** (-5.0 - hd))


def _retention_constants(C):
    lg = jnp.asarray([_log_gamma(hd) for hd in range(RET_HEADS)], _F32)
    idx = jnp.arange(C, dtype=_F32)
    diff = idx[:, None] - idx[None, :]
    decay = jnp.where(diff >= 0, jnp.exp(lg[:, None, None] * jnp.maximum(diff, 0.0)), 0.0)
    cross = jnp.exp(lg[:, None] * (idx[None, :] + 1.0))
    kdec = jnp.exp(lg[:, None] * (C - 1.0 - idx[None, :]))
    bcast = lambda a: jnp.broadcast_to(a[:, :, None], (RET_HEADS, C, HEAD_DIM))
    return decay * Q_SCALE, bcast(cross * Q_SCALE), bcast(kdec)


def _rope_tables(pos):
    half = HEAD_DIM // 2
    inv_freq = ROPE_BASE ** (-jnp.arange(half, dtype=_F32) / half)
    ang = pos[:, None] * inv_freq[None, :]
    cos, sin = jnp.cos(ang), jnp.sin(ang)
    return jnp.concatenate([cos, cos], axis=-1), jnp.concatenate([-sin, sin], axis=-1)


def _const_spec(shape):
    return pl.BlockSpec(shape, lambda b, t: (0,) * len(shape), pipeline_mode=pl.Buffered(1))


def _layer_spec(shape, layer, n_grid):
    index = (lambda b, t: (layer, 0, 0)) if n_grid == 2 else (lambda r: (layer, 0, 0))
    return pl.BlockSpec((None,) + shape, index, pipeline_mode=pl.Buffered(1))


def _mixer(x2d, s0, lw, tables, *, layer, n_streams, S, nb, T, C, n_sub, want_gv):
    has_state = s0 is not None
    assert S % T == 0 and T % C == 0 and n_streams % nb == 0
    assert C == min(S, GM_CHUNK)
    assert nb == 1 or S == T
    R = nb * T
    n_t = S // T
    cos, sin, decay, cross, kdec = tables
    gmw = lw['gm_w'][:, :C, :C]
    gmb = jnp.broadcast_to(lw['gm_b'][:, :C, None], (GM_GROUPS, C, HEAD_DIM))

    row_spec = lambda width: pl.BlockSpec((R, width), lambda b, t: (b * n_t + t, 0))
    state_spec = pl.BlockSpec((nb, RET_HEADS, HEAD_DIM, HEAD_DIM), lambda b, t: (b, 0, 0, 0))

    in_specs = [row_spec(D_MODEL)]
    args = [x2d]
    if has_state:
        in_specs.append(state_spec)
        args.append(s0)
    in_specs += [
        _const_spec((1, D_MODEL)),
        _layer_spec((D_MODEL, 6 * RET_WIDTH), layer, 2),
        pl.BlockSpec((R, HEAD_DIM), lambda b, t: (t, 0)),
        pl.BlockSpec((R, HEAD_DIM), lambda b, t: (t, 0)),
        _const_spec((RET_HEADS, C, C)),
        _const_spec((RET_HEADS, C, HEAD_DIM)),
        _const_spec((RET_HEADS, C, HEAD_DIM)),
        _const_spec((1, RET_WIDTH)),
        _const_spec((1, GM_WIDTH)),
        _const_spec((1, GM_WIDTH)),
        _const_spec((GM_GROUPS, C, C)),
        _const_spec((GM_GROUPS, C, HEAD_DIM)),
        _layer_spec((D_MODEL, D_MODEL), layer, 2),
    ]
    args += [lw['g_mix'], lw['w_in'], cos, sin, decay, cross, kdec, lw['ret_gn'],
             lw['gm_ln_g'], lw['gm_ln_b'], gmw, gmb, lw['w_out']]

    out_shape = [jax.ShapeDtypeStruct(x2d.shape, _F32),
                 jax.ShapeDtypeStruct((n_streams, RET_HEADS, HEAD_DIM, HEAD_DIM), _F32)]
    out_specs = [row_spec(D_MODEL), state_spec]
    if want_gv:
        out_shape.append(jax.ShapeDtypeStruct((x2d.shape[0], GM_WIDTH), _F32))
        out_specs.append(row_spec(GM_WIDTH))

    scratch = [pltpu.VMEM((R, RET_WIDTH), _BF16),
               pltpu.VMEM((R, RET_WIDTH), _BF16),
               pltpu.VMEM((R, RET_WIDTH), _BF16),
               pltpu.VMEM((R, RET_WIDTH), _F32),
               pltpu.VMEM((R, GM_WIDTH), _F32),
               pltpu.VMEM((R, GM_WIDTH), _BF16),
               pltpu.VMEM((R, D_MODEL), _BF16)]

    kern = functools.partial(_mixer_kernel, nb=nb, T=T, C=C, n_sub=n_sub,
                             has_state=has_state, want_gv=want_gv)
    return pl.pallas_call(
        kern,
        grid=(n_streams // nb, n_t),
        in_specs=in_specs,
        out_specs=out_specs,
        out_shape=out_shape,
        scratch_shapes=scratch,
        compiler_params=pltpu.CompilerParams(
            dimension_semantics=("arbitrary", "arbitrary"),
            vmem_limit_bytes=_VMEM_LIMIT_BYTES),
        name="mixer_state" if has_state else "mixer",
    )(*args)


def _ffn(x2d, lw, g_final, *, layer, R, final):
    n_rows = x2d.shape[0]
    assert n_rows % R == 0
    n_sub = max(1, R // _FFN_GROUP_ROWS)
    const = lambda shape: pl.BlockSpec(shape, lambda r: (0,) * len(shape),
                                       pipeline_mode=pl.Buffered(1))
    row_spec = pl.BlockSpec((R, D_MODEL), lambda r: (r, 0))
    return pl.pallas_call(
        functools.partial(_ffn_kernel, n_sub=n_sub, final=final),
        grid=(n_rows // R,),
        in_specs=[row_spec, const((1, D_MODEL)), _layer_spec((D_MODEL, D_FF), layer, 1),
                  _layer_spec((D_FF, D_MODEL), layer, 1), const((1, D_MODEL))],
        out_specs=row_spec,
        out_shape=jax.ShapeDtypeStruct(x2d.shape, _F32),
        scratch_shapes=[pltpu.VMEM((R, D_FF), _BF16)],
        compiler_params=pltpu.CompilerParams(
            dimension_semantics=("arbitrary",),
            vmem_limit_bytes=_VMEM_LIMIT_BYTES),
        name="ffn_final" if final else "ffn",
    )(x2d, lw['g_ffn'], lw['w_up'], lw['w_down'], g_final)


def _pick_tile(S, pref):
    t = min(S, pref)
    while S % t:
        t //= 2
    return t


def _forward(x_prompt, x_sample, state_ret, g_mix, w_in, ret_gn, gm_ln_g, gm_ln_b, gm_w, gm_b,
             w_out, g_ffn, w_up, w_down, g_final):
    B, S, D = x_prompt.shape
    DB, DS, _ = x_sample.shape
    depth = w_in.shape[0]

    T_p = _pick_tile(S, _PROMPT_TILE_ROWS)
    C_p = min(S, GM_CHUNK)
    sub_p = max(1, T_p // _PIPELINE_GROUP_ROWS)
    nb_s = math.gcd(DB, _SAMPLE_STREAMS_PER_TILE)

    cos_p, sin_p = _rope_tables(jnp.arange(S, dtype=_F32))
    cos_s, sin_s = _rope_tables(PAST_LEN + jnp.arange(DS, dtype=_F32))
    cos_s, sin_s = jnp.tile(cos_s, (nb_s, 1)), jnp.tile(sin_s, (nb_s, 1))
    tab_p = (cos_p, sin_p) + _retention_constants(C_p)
    tab_s = (cos_s, sin_s) + _retention_constants(DS)

    row = lambda a: a.reshape(1, -1)
    g_fin = row(g_final)
    hp = x_prompt.reshape(B * S, D)
    hs = x_sample.reshape(DB * DS, D)
    R_ffn_p = _pick_tile(B * S, _FFN_TILE_ROWS)
    R_ffn_s = _pick_tile(DB * DS, _FFN_TILE_ROWS)

    w_in_b, w_out_b = w_in.astype(_BF16), w_out.astype(_BF16)
    w_up_b, w_down_b = w_up.astype(_BF16), w_down.astype(_BF16)

    sp, ss, gvs = [], [], []
    for l in range(depth):
        lw = dict(g_mix=row(g_mix[l]), w_in=w_in_b, ret_gn=row(ret_gn[l]),
                  gm_ln_g=row(gm_ln_g[l]), gm_ln_b=row(gm_ln_b[l]), gm_w=gm_w[l], gm_b=gm_b[l],
                  w_out=w_out_b, g_ffn=row(g_ffn[l]), w_up=w_up_b, w_down=w_down_b)
        final = l == depth - 1
        hp, s_p = _mixer(hp, None, lw, tab_p, layer=l, n_streams=B, S=S, nb=1, T=T_p, C=C_p,
                         n_sub=sub_p, want_gv=False)
        hp = _ffn(hp, lw, g_fin, layer=l, R=R_ffn_p, final=final)
        hs, s_s, gv_s = _mixer(hs, state_ret[l], lw, tab_s, layer=l, n_streams=DB, S=DS, nb=nb_s,
                               T=DS, C=DS, n_sub=1, want_gv=True)
        hs = _ffn(hs, lw, g_fin, layer=l, R=R_ffn_s, final=final)
        sp.append(s_p)
        ss.append(s_s)
        gvs.append(gv_s.reshape(DB, DS, GM_WIDTH))

    return (hp.reshape(B, S, D), hs.reshape(DB, DS, D),
            jnp.stack(sp), jnp.stack(ss), jnp.stack(gvs))


def kernel(x_prompt, x_sample, state_ret, g_mix, w_in, ret_gn, gm_ln_g, gm_ln_b, gm_w, gm_b,
           w_out, g_ffn, w_up, w_down, g_final):
    return _forward(x_prompt, x_sample, state_ret, g_mix, w_in, ret_gn, gm_ln_g, gm_ln_b, gm_w,
                    gm_b, w_out, g_ffn, w_up, w_down, g_final)
```

```python
import functools
import math

import jax
import jax.numpy as jnp
from jax import lax
from jax.experimental import pallas as pl
from jax.experimental.pallas import tpu as pltpu

D_MODEL = 1024
PAST_LEN = 4096
RET_HEADS = 4
HEAD_DIM = 128
RET_WIDTH = RET_HEADS * HEAD_DIM
GM_GROUPS = 4
GM_WIDTH = GM_GROUPS * HEAD_DIM
GM_CHUNK = 128
D_FF = 4 * D_MODEL
ROPE_BASE = 10000.0
EPS = 1e-6
Q_SCALE = HEAD_DIM ** -0.5

_OFF_Q, _OFF_K, _OFF_V, _OFF_GATE, _OFF_U, _OFF_GV = (i * RET_WIDTH for i in range(6))

_VMEM_LIMIT_BYTES = 56 * 1024 * 1024
_FFN_COL_CHUNK = 1024
_FFN_TILE_ROWS = 1024
_FFN_GROUP_ROWS = 512
_PROMPT_TILE_ROWS = 1024
_PIPELINE_GROUP_ROWS = 512
_SAMPLE_STREAMS_PER_TILE = 32
_SMALL_TASK_BATCH = 4

_BF16 = jnp.bfloat16
_F32 = jnp.float32


def _rms_scale(x, g):
    ms = jnp.mean(x * x, axis=-1, keepdims=True)
    return x * lax.rsqrt(ms + EPS) * g


def _gelu_tanh(x):
    c = math.sqrt(2.0 / math.pi)
    return 0.5 * x * (1.0 + jnp.tanh(c * (x + 0.044715 * (x * x * x))))


def _dot(a, b):
    return jnp.dot(a, b, preferred_element_type=_F32)


def _interleave(first, second):
    if not second:
        return list(first)
    if not first:
        return list(second)
    out, j = [], 0
    for i, task in enumerate(first):
        out.append(task)
        want = ((i + 1) * len(second)) // len(first)
        while j < want:
            out.append(second[j])
            j += 1
    return out


def _mixer_kernel(*refs, nb, T, C, n_sub, has_state, want_gv):
    it = iter(refs)
    x_ref = next(it)
    s0_ref = next(it) if has_state else None
    (gmix_ref, win_ref, cos_ref, sin_ref, decay_ref, cross_ref, kdec_ref, retgn_ref,
     lng_ref, lnb_ref, gmw_ref, gmb_ref, wout_ref) = (next(it) for _ in range(13))
    xo_ref = next(it)
    so_ref = next(it)
    gv_ref = next(it) if want_gv else None
    q_s, k_s, v_s, gate_s, u_s, gvb_s, mix_s = (next(it) for _ in range(7))

    R = nb * T
    M = R // n_sub
    assert M % C == 0
    t = pl.program_id(1)

    @pl.when(t == 0)
    def _():
        if has_state:
            so_ref[...] = s0_ref[...]
        else:
            so_ref[...] = jnp.zeros_like(so_ref)

    row_id = lax.broadcasted_iota(jnp.int32, (C, C), 0)
    col_id = lax.broadcasted_iota(jnp.int32, (C, C), 1)
    causal = row_id >= col_id
    ws = [jnp.where(causal, gmw_ref[g], 0.0).astype(_BF16) for g in range(GM_GROUPS)]

    def projection_tasks(j):
        rows = slice(j * M, (j + 1) * M)
        val = {}

        def norm():
            val['h'] = _rms_scale(x_ref[rows, :], gmix_ref[...]).astype(_BF16)

        def proj(name, off):
            def task():
                val[name] = _dot(val['h'], win_ref[:, off:off + RET_WIDTH])
            return task

        def rope_to(dst, name):
            def task():
                cos = cos_ref[rows, :]
                sin = sin_ref[rows, :]
                p = val.pop(name)
                for hd in range(RET_HEADS):
                    sl = slice(hd * HEAD_DIM, (hd + 1) * HEAD_DIM)
                    ph = p[:, sl]
                    dst[rows, sl] = (ph * cos + pltpu.roll(ph, HEAD_DIM // 2, 1) * sin).astype(_BF16)
            return task

        def u_act():
            u_s[rows, :] = _gelu_tanh(val.pop('u'))

        def gv_act():
            gvp = _gelu_tanh(val.pop('gv'))
            mu = jnp.mean(gvp, axis=-1, keepdims=True)
            gc = gvp - mu
            gv = gc * lax.rsqrt(jnp.mean(gc * gc, axis=-1, keepdims=True) + EPS)
            gv = gv * lng_ref[...] + lnb_ref[...]
            if want_gv:
                gv_ref[rows, :] = gv
            gvb_s[rows, :] = gv.astype(_BF16)

        def gate_act():
            pgate = val.pop('gate')
            gate_s[rows, :] = pgate * jax.nn.sigmoid(pgate) * retgn_ref[...]

        def v_act():
            v_s[rows, :] = val.pop('v').astype(_BF16)

        return [[norm, proj('k', _OFF_K)],
                [proj('q', _OFF_Q)],
                [proj('v', _OFF_V), rope_to(k_s, 'k')],
                [proj('gate', _OFF_GATE), rope_to(q_s, 'q')],
                [proj('u', _OFF_U), v_act],
                [proj('gv', _OFF_GV), gate_act, u_act, gv_act]]

    def retention_scores(stream, r0, hd, val):
        rows = slice(r0, r0 + C)
        sl = slice(hd * HEAD_DIM, (hd + 1) * HEAD_DIM)
        qh = q_s[rows, sl]
        kh = k_s[rows, sl]
        state = so_ref[stream, hd]
        val['sc'] = lax.dot_general(qh, kh, (((1,), (1,)), ((), ())),
                                    preferred_element_type=_F32)
        val['cross'] = _dot(qh, state.astype(_BF16))
        kd = (kh.astype(_F32) * kdec_ref[hd]).astype(_BF16)
        kv = lax.dot_general(kd, v_s[rows, sl], (((0,), (0,)), ((), ())),
                             preferred_element_type=_F32)
        so_ref[stream, hd] = state * math.exp(_log_gamma(hd) * C) + kv

    def retention_output(r0, hd, val):
        rows = slice(r0, r0 + C)
        sl = slice(hd * HEAD_DIM, (hd + 1) * HEAD_DIM)
        sc = (val.pop('sc') * decay_ref[hd]).astype(_BF16)
        o = _dot(sc, v_s[rows, sl]) + val.pop('cross') * cross_ref[hd]
        on = o * lax.rsqrt(jnp.mean(o * o, axis=-1, keepdims=True) + EPS)
        mix_s[rows, sl] = (gate_s[rows, sl] * on).astype(_BF16)

    def gmlp_block(r0, g):
        rows = slice(r0, r0 + C)
        sl = slice(g * HEAD_DIM, (g + 1) * HEAD_DIM)
        osl = slice(RET_WIDTH + g * HEAD_DIM, RET_WIDTH + (g + 1) * HEAD_DIM)
        z = _dot(ws[g], gvb_s[rows, sl]) + gmb_ref[g]
        mix_s[rows, osl] = (u_s[rows, sl] * z).astype(_BF16)

    def chunk_items(j):
        items = []
        for r0 in range(j * M, (j + 1) * M, C):
            for hd in range(RET_HEADS):
                val = {}
                items.append((functools.partial(retention_scores, r0 // T, r0, hd, val),
                              [functools.partial(gmlp_block, r0, hd),
                               functools.partial(retention_output, r0, hd, val)]))
        return items

    def output_tasks(j):
        rows = slice(j * M, (j + 1) * M)
        val = {}

        def first():
            val['y'] = _dot(mix_s[rows, :RET_WIDTH], wout_ref[:RET_WIDTH, :])

        def second():
            y = val.pop('y') + _dot(mix_s[rows, RET_WIDTH:], wout_ref[RET_WIDTH:, :])
            xo_ref[rows, :] = x_ref[rows, :] + y

        return [[first], [second]]

    program = []
    for step in range(n_sub + 2):
        bigs = _interleave(projection_tasks(step) if step < n_sub else [],
                           output_tasks(step - 2) if 0 <= step - 2 < n_sub else [])
        items = chunk_items(step - 1) if 0 <= step - 1 < n_sub else []
        n_slots = max(len(bigs), -(-len(items) // _SMALL_TASK_BATCH))
        per = -(-len(items) // n_slots) if items else 0
        pending = []
        for slot in range(n_slots):
            if slot < len(bigs):
                program += bigs[slot]
            batch = items[slot * per:(slot + 1) * per]
            program += [scores for scores, _ in batch]
            program += pending
            pending = [task for _, later in batch for task in later]
        program += pending
    for task in program:
        task()


def _ffn_kernel(x_ref, g_ref, wup_ref, wdn_ref, gfin_ref, o_ref, hid_s, *, n_sub, final):
    M = x_ref.shape[0] // n_sub
    n_col = D_FF // _FFN_COL_CHUNK

    def group_tasks(j):
        rows = slice(j * M, (j + 1) * M)
        val = {}

        def norm():
            val['h'] = _rms_scale(x_ref[rows, :], g_ref[...]).astype(_BF16)

        def up(n):
            def task():
                cols = slice(n * _FFN_COL_CHUNK, (n + 1) * _FFN_COL_CHUNK)
                a = jnp.maximum(_dot(val['h'], wup_ref[:, cols]), 0.0)
                hid_s[rows, cols] = (a * a).astype(_BF16)
            return task

        def down():
            y = x_ref[rows, :] + _dot(hid_s[rows, :], wdn_ref[...])
            if final:
                y = _rms_scale(y, gfin_ref[...])
            o_ref[rows, :] = y

        return norm, [up(n) for n in range(n_col)] + [down]

    groups = [group_tasks(j) for j in range(n_sub)]
    groups[0][0]()
    for j, (_, matmuls) in enumerate(groups):
        for n, task in enumerate(matmuls):
            task()
            if n == 0 and j + 1 < n_sub:
                groups[j + 1][0]()


def _log_gamma(hd):
    return math.log(1.0 - 2.0 ** (-5.0 - hd))


def _retention_constants(C):
    lg = jnp.asarray([_log_gamma(hd) for hd in range(RET_HEADS)], _F32)
    idx = jnp.arange(C, dtype=_F32)
    diff = idx[:, None] - idx[None, :]
    decay = jnp.where(diff >= 0, jnp.exp(lg[:, None, None] * jnp.maximum(diff, 0.0)), 0.0)
    cross = jnp.exp(lg[:, None] * (idx[None, :] + 1.0))
    kdec = jnp.exp(lg[:, None] * (C - 1.0 - idx[None, :]))
    bcast = lambda a: jnp.broadcast_to(a[:, :, None], (RET_HEADS, C, HEAD_DIM))
    return decay * Q_SCALE, bcast(cross * Q_SCALE), bcast(kdec)


def _rope_tables(pos):
    half = HEAD_DIM // 2
    inv_freq = ROPE_BASE ** (-jnp.arange(half, dtype=_F32) / half)
    ang = pos[:, None] * inv_freq[None, :]
    cos, sin = jnp.cos(ang), jnp.sin(ang)
    return jnp.concatenate([cos, cos], axis=-1), jnp.concatenate([-sin, sin], axis=-1)


def _const_spec(shape):
    return pl.BlockSpec(shape, lambda b, t: (0,) * len(shape), pipeline_mode=pl.Buffered(1))


def _layer_spec(shape, layer, n_grid):
    index = (lambda b, t: (layer, 0, 0)) if n_grid == 2 else (lambda r: (layer, 0, 0))
    return pl.BlockSpec((None,) + shape, index, pipeline_mode=pl.Buffered(1))


def _mixer(x2d, s0, lw, tables, *, layer, n_streams, S, nb, T, C, n_sub, want_gv):
    has_state = s0 is not None
    assert S % T == 0 and T % C == 0 and n_streams % nb == 0
    assert C == min(S, GM_CHUNK)
    assert nb == 1 or S == T
    R = nb * T
    n_t = S // T
    cos, sin, decay, cross, kdec = tables
    gmw = lw['gm_w'][:, :C, :C]
    gmb = jnp.broadcast_to(lw['gm_b'][:, :C, None], (GM_GROUPS, C, HEAD_DIM))

    row_spec = lambda width: pl.BlockSpec((R, width), lambda b, t: (b * n_t + t, 0))
    state_spec = pl.BlockSpec((nb, RET_HEADS, HEAD_DIM, HEAD_DIM), lambda b, t: (b, 0, 0, 0))

    in_specs = [row_spec(D_MODEL)]
    args = [x2d]
    if has_state:
        in_specs.append(state_spec)
        args.append(s0)
    in_specs += [
        _const_spec((1, D_MODEL)),
        _layer_spec((D_MODEL, 6 * RET_WIDTH), layer, 2),
        pl.BlockSpec((R, HEAD_DIM), lambda b, t: (t, 0)),
        pl.BlockSpec((R, HEAD_DIM), lambda b, t: (t, 0)),
        _const_spec((RET_HEADS, C, C)),
        _const_spec((RET_HEADS, C, HEAD_DIM)),
        _const_spec((RET_HEADS, C, HEAD_DIM)),
        _const_spec((1, RET_WIDTH)),
        _const_spec((1, GM_WIDTH)),
        _const_spec((1, GM_WIDTH)),
        _const_spec((GM_GROUPS, C, C)),
        _const_spec((GM_GROUPS, C, HEAD_DIM)),
        _layer_spec((D_MODEL, D_MODEL), layer, 2),
    ]
    args += [lw['g_mix'], lw['w_in'], cos, sin, decay, cross, kdec, lw['ret_gn'],
             lw['gm_ln_g'], lw['gm_ln_b'], gmw, gmb, lw['w_out']]

    out_shape = [jax.ShapeDtypeStruct(x2d.shape, _F32),
                 jax.ShapeDtypeStruct((n_streams, RET_HEADS, HEAD_DIM, HEAD_DIM), _F32)]
    out_specs = [row_spec(D_MODEL), state_spec]
    if want_gv:
        out_shape.append(jax.ShapeDtypeStruct((x2d.shape[0], GM_WIDTH), _F32))
        out_specs.append(row_spec(GM_WIDTH))

    scratch = [pltpu.VMEM((R, RET_WIDTH), _BF16),
               pltpu.VMEM((R, RET_WIDTH), _BF16),
               pltpu.VMEM((R, RET_WIDTH), _BF16),
               pltpu.VMEM((R, RET_WIDTH), _F32),
               pltpu.VMEM((R, GM_WIDTH), _F32),
               pltpu.VMEM((R, GM_WIDTH), _BF16),
               pltpu.VMEM((R, D_MODEL), _BF16)]

    kern = functools.partial(_mixer_kernel, nb=nb, T=T, C=C, n_sub=n_sub,
                             has_state=has_state, want_gv=want_gv)
    return pl.pallas_call(
        kern,
        grid=(n_streams // nb, n_t),
        in_specs=in_specs,
        out_specs=out_specs,
        out_shape=out_shape,
        scratch_shapes=scratch,
        compiler_params=pltpu.CompilerParams(
            dimension_semantics=("arbitrary", "arbitrary"),
            vmem_limit_bytes=_VMEM_LIMIT_BYTES),
        name="mixer_state" if has_state else "mixer",
    )(*args)


def _ffn(x2d, lw, g_final, *, layer, R, final):
    n_rows = x2d.shape[0]
    assert n_rows % R == 0
    n_sub = max(1, R // _FFN_GROUP_ROWS)
    const = lambda shape: pl.BlockSpec(shape, lambda r: (0,) * len(shape),
                                       pipeline_mode=pl.Buffered(1))
    row_spec = pl.BlockSpec((R, D_MODEL), lambda r: (r, 0))
    return pl.pallas_call(
        functools.partial(_ffn_kernel, n_sub=n_sub, final=final),
        grid=(n_rows // R,),
        in_specs=[row_spec, const((1, D_MODEL)), _layer_spec((D_MODEL, D_FF), layer, 1),
                  _layer_spec((D_FF, D_MODEL), layer, 1), const((1, D_MODEL))],
        out_specs=row_spec,
        out_shape=jax.ShapeDtypeStruct(x2d.shape, _F32),
        scratch_shapes=[pltpu.VMEM((R, D_FF), _BF16)],
        compiler_params=pltpu.CompilerParams(
            dimension_semantics=("arbitrary",),
            vmem_limit_bytes=_VMEM_LIMIT_BYTES),
        name="ffn_final" if final else "ffn",
    )(x2d, lw['g_ffn'], lw['w_up'], lw['w_down'], g_final)


def _pick_tile(S, pref):
    t = min(S, pref)
    while S % t:
        t //= 2
    return t


def _forward(x_prompt, x_sample, state_ret, g_mix, w_in, ret_gn, gm_ln_g, gm_ln_b, gm_w, gm_b,
             w_out, g_ffn, w_up, w_down, g_final):
    B, S, D = x_prompt.shape
    DB, DS, _ = x_sample.shape
    depth = w_in.shape[0]

    T_p = _pick_tile(S, _PROMPT_TILE_ROWS)
    C_p = min(S, GM_CHUNK)
    sub_p = max(1, T_p // _PIPELINE_GROUP_ROWS)
    nb_s = math.gcd(DB, _SAMPLE_STREAMS_PER_TILE)
    sub_s = max(1, (nb_s * DS) // _PIPELINE_GROUP_ROWS)

    cos_p, sin_p = _rope_tables(jnp.arange(S, dtype=_F32))
    cos_s, sin_s = _rope_tables(PAST_LEN + jnp.arange(DS, dtype=_F32))
    cos_s, sin_s = jnp.tile(cos_s, (nb_s, 1)), jnp.tile(sin_s, (nb_s, 1))
    tab_p = (cos_p, sin_p) + _retention_constants(C_p)
    tab_s = (cos_s, sin_s) + _retention_constants(DS)

    row = lambda a: a.reshape(1, -1)
    g_fin = row(g_final)
    hp = x_prompt.reshape(B * S, D)
    hs = x_sample.reshape(DB * DS, D)
    R_ffn_p = _pick_tile(B * S, _FFN_TILE_ROWS)
    R_ffn_s = _pick_tile(DB * DS, _FFN_TILE_ROWS)

    w_in_b, w_out_b = w_in.astype(_BF16), w_out.astype(_BF16)
    w_up_b, w_down_b = w_up.astype(_BF16), w_down.astype(_BF16)

    sp, ss, gvs = [], [], []
    for l in range(depth):
        lw = dict(g_mix=row(g_mix[l]), w_in=w_in_b, ret_gn=row(ret_gn[l]),
                  gm_ln_g=row(gm_ln_g[l]), gm_ln_b=row(gm_ln_b[l]), gm_w=gm_w[l], gm_b=gm_b[l],
                  w_out=w_out_b, g_ffn=row(g_ffn[l]), w_up=w_up_b, w_down=w_down_b)
        final = l == depth - 1
        hp, s_p = _mixer(hp, None, lw, tab_p, layer=l, n_streams=B, S=S, nb=1, T=T_p, C=C_p,
                         n_sub=sub_p, want_gv=False)
        hp = _ffn(hp, lw, g_fin, layer=l, R=R_ffn_p, final=final)
        hs, s_s, gv_s = _mixer(hs, state_ret[l], lw, tab_s, layer=l, n_streams=DB, S=DS, nb=nb_s,
                               T=DS, C=DS, n_sub=sub_s, want_gv=True)
        hs = _ffn(hs, lw, g_fin, layer=l, R=R_ffn_s, final=final)
        sp.append(s_p)
        ss.append(s_s)
        gvs.append(gv_s.reshape(DB, DS, GM_WIDTH))

    return (hp.reshape(B, S, D), hs.reshape(DB, DS, D),
            jnp.stack(sp), jnp.stack(ss), jnp.stack(gvs))


def kernel(x_prompt, x_sample, state_ret, g_mix, w_in, ret_gn, gm_ln_g, gm_ln_b, gm_w, gm_b,
           w_out, g_ffn, w_up, w_down, g_final):
    return _forward(x_prompt, x_sample, state_ret, g_mix, w_in, ret_gn, gm_ln_g, gm_ln_b, gm_w,
                    gm_b, w_out, g_ffn, w_up, w_down, g_final)
```

```python
import functools
import math

import jax
import jax.numpy as jnp
from jax import lax
from jax.experimental import pallas as pl
from jax.experimental.pallas import tpu as pltpu

D_MODEL = 1024
PAST_LEN = 4096
RET_HEADS = 4
HEAD_DIM = 128
RET_WIDTH = RET_HEADS * HEAD_DIM
GM_GROUPS = 4
GM_WIDTH = GM_GROUPS * HEAD_DIM
GM_CHUNK = 128
D_FF = 4 * D_MODEL
ROPE_BASE = 10000.0
EPS = 1e-6
Q_SCALE = HEAD_DIM ** -0.5

_OFF_Q, _OFF_K, _OFF_V, _OFF_GATE, _OFF_U, _OFF_GV = (i * RET_WIDTH for i in range(6))

_VMEM_LIMIT_BYTES = 56 * 1024 * 1024
_FFN_COL_CHUNK = 1024
_FFN_TILE_ROWS = 1024
_FFN_GROUP_ROWS = 512
_PROMPT_TILE_ROWS = 1024
_PROMPT_GROUP_ROWS = (512, 512)
_SAMPLE_GROUP_ROWS = (512, 512)
_SAMPLE_STREAMS_PER_TILE = 32
_SMALL_TASK_BATCH = 4

_BF16 = jnp.bfloat16
_F32 = jnp.float32


def _rms_scale(x, g):
    ms = jnp.mean(x * x, axis=-1, keepdims=True)
    return x * lax.rsqrt(ms + EPS) * g


def _gelu_tanh(x):
    c = math.sqrt(2.0 / math.pi)
    return 0.5 * x * (1.0 + jnp.tanh(c * (x + 0.044715 * (x * x * x))))


def _dot(a, b):
    return jnp.dot(a, b, preferred_element_type=_F32)


def _interleave(first, second):
    if not second:
        return list(first)
    if not first:
        return list(second)
    out, j = [], 0
    for i, task in enumerate(first):
        out.append(task)
        want = ((i + 1) * len(second)) // len(first)
        while j < want:
            out.append(second[j])
            j += 1
    return out


def _mixer_kernel(*refs, nb, T, C, group_rows, has_state, want_gv):
    it = iter(refs)
    x_ref = next(it)
    s0_ref = next(it) if has_state else None
    (gmix_ref, win_ref, cos_ref, sin_ref, decay_ref, cross_ref, kdec_ref, retgn_ref,
     lng_ref, lnb_ref, gmw_ref, gmb_ref, wout_ref) = (next(it) for _ in range(13))
    for _ in range(2 if want_gv else 1):
        next(it)
    xo_ref = next(it)
    so_ref = next(it)
    gv_ref = next(it) if want_gv else None
    q_s, k_s, v_s, gate_s, u_s, gvb_s, mix_s = (next(it) for _ in range(7))

    R = nb * T
    n_sub = len(group_rows)
    bounds = [sum(group_rows[:j]) for j in range(n_sub + 1)]
    assert bounds[-1] == R and all(rows % C == 0 for rows in group_rows)

    @pl.when(pl.program_id(1) == 0)
    def _():
        if has_state:
            so_ref[...] = s0_ref[...]
        else:
            so_ref[...] = jnp.zeros_like(so_ref)

    row_id = lax.broadcasted_iota(jnp.int32, (C, C), 0)
    col_id = lax.broadcasted_iota(jnp.int32, (C, C), 1)
    causal = row_id >= col_id
    ws = [jnp.where(causal, gmw_ref[g], 0.0).astype(_BF16) for g in range(GM_GROUPS)]

    def projection_tasks(j):
        rows = slice(bounds[j], bounds[j + 1])
        val = {}

        def norm():
            val['h'] = _rms_scale(x_ref[rows, :], gmix_ref[...]).astype(_BF16)

        def proj(name, off):
            def task():
                val[name] = _dot(val['h'], win_ref[:, off:off + RET_WIDTH])
            return task

        def rope_to(dst, name):
            def task():
                cos = cos_ref[rows, :]
                sin = sin_ref[rows, :]
                p = val.pop(name)
                for hd in range(RET_HEADS):
                    sl = slice(hd * HEAD_DIM, (hd + 1) * HEAD_DIM)
                    ph = p[:, sl]
                    dst[rows, sl] = (ph * cos + pltpu.roll(ph, HEAD_DIM // 2, 1) * sin).astype(_BF16)
            return task

        def u_act():
            u_s[rows, :] = _gelu_tanh(val.pop('u'))

        def gv_act():
            gvp = _gelu_tanh(val.pop('gv'))
            mu = jnp.mean(gvp, axis=-1, keepdims=True)
            gc = gvp - mu
            gv = gc * lax.rsqrt(jnp.mean(gc * gc, axis=-1, keepdims=True) + EPS)
            gv = gv * lng_ref[...] + lnb_ref[...]
            if want_gv:
                gv_ref[rows, :] = gv
            gvb_s[rows, :] = gv.astype(_BF16)

        def gate_act():
            pgate = val.pop('gate')
            gate_s[rows, :] = pgate * jax.nn.sigmoid(pgate) * retgn_ref[...]

        def v_act():
            v_s[rows, :] = val.pop('v').astype(_BF16)

        return [[norm, proj('k', _OFF_K)],
                [proj('q', _OFF_Q)],
                [proj('v', _OFF_V), rope_to(k_s, 'k')],
                [proj('gate', _OFF_GATE), rope_to(q_s, 'q')],
                [proj('u', _OFF_U), v_act],
                [proj('gv', _OFF_GV), gate_act, u_act, gv_act]]

    def retention_scores(stream, r0, hd, val):
        rows = slice(r0, r0 + C)
        sl = slice(hd * HEAD_DIM, (hd + 1) * HEAD_DIM)
        qh = q_s[rows, sl]
        kh = k_s[rows, sl]
        state = so_ref[stream, hd]
        val['sc'] = lax.dot_general(qh, kh, (((1,), (1,)), ((), ())),
                                    preferred_element_type=_F32)
        val['cross'] = _dot(qh, state.astype(_BF16))
        kd = (kh.astype(_F32) * kdec_ref[hd]).astype(_BF16)
        kv = lax.dot_general(kd, v_s[rows, sl], (((0,), (0,)), ((), ())),
                             preferred_element_type=_F32)
        so_ref[stream, hd] = state * math.exp(_log_gamma(hd) * C) + kv

    def retention_output(r0, hd, val):
        rows = slice(r0, r0 + C)
        sl = slice(hd * HEAD_DIM, (hd + 1) * HEAD_DIM)
        sc = (val.pop('sc') * decay_ref[hd]).astype(_BF16)
        o = _dot(sc, v_s[rows, sl]) + val.pop('cross') * cross_ref[hd]
        on = o * lax.rsqrt(jnp.mean(o * o, axis=-1, keepdims=True) + EPS)
        mix_s[rows, sl] = (gate_s[rows, sl] * on).astype(_BF16)

    def gmlp_block(r0, g):
        rows = slice(r0, r0 + C)
        sl = slice(g * HEAD_DIM, (g + 1) * HEAD_DIM)
        osl = slice(RET_WIDTH + g * HEAD_DIM, RET_WIDTH + (g + 1) * HEAD_DIM)
        z = _dot(ws[g], gvb_s[rows, sl]) + gmb_ref[g]
        mix_s[rows, osl] = (u_s[rows, sl] * z).astype(_BF16)

    def chunk_items(j):
        items = []
        for r0 in range(bounds[j], bounds[j + 1], C):
            for hd in range(RET_HEADS):
                val = {}
                items.append((functools.partial(retention_scores, r0 // T, r0, hd, val),
                              [functools.partial(gmlp_block, r0, hd),
                               functools.partial(retention_output, r0, hd, val)]))
        return items

    def output_tasks(j):
        rows = slice(bounds[j], bounds[j + 1])
        val = {}

        def first():
            val['y'] = _dot(mix_s[rows, :RET_WIDTH], wout_ref[:RET_WIDTH, :])

        def second():
            y = val.pop('y') + _dot(mix_s[rows, RET_WIDTH:], wout_ref[RET_WIDTH:, :])
            xo_ref[rows, :] = x_ref[rows, :] + y

        return [[first], [second]]

    def phase(bigs, items):
        tasks = []
        n_slots = max(len(bigs), -(-len(items) // _SMALL_TASK_BATCH))
        per = -(-len(items) // n_slots) if items else 0
        pending = []
        for slot in range(n_slots):
            if slot < len(bigs):
                tasks += bigs[slot]
            batch = items[slot * per:(slot + 1) * per]
            tasks += [scores for scores, _ in batch]
            tasks += pending
            pending = [task for _, later in batch for task in later]
        return tasks + pending

    program = []
    for p in range(n_sub + 2):
        bigs = _interleave(projection_tasks(p) if p < n_sub else [],
                           output_tasks(p - 2) if 0 <= p - 2 < n_sub else [])
        program += phase(bigs, chunk_items(p - 1) if 0 <= p - 1 < n_sub else [])
    for task in program:
        task()


def _ffn_kernel(x_ref, g_ref, wup_ref, wdn_ref, gfin_ref, o_ref, hid_s, *, n_sub, final):
    M = x_ref.shape[0] // n_sub
    n_col = D_FF // _FFN_COL_CHUNK

    def group_tasks(j):
        rows = slice(j * M, (j + 1) * M)
        val = {}

        def norm():
            val['h'] = _rms_scale(x_ref[rows, :], g_ref[...]).astype(_BF16)

        def up(n):
            def task():
                cols = slice(n * _FFN_COL_CHUNK, (n + 1) * _FFN_COL_CHUNK)
                a = jnp.maximum(_dot(val['h'], wup_ref[:, cols]), 0.0)
                hid_s[rows, cols] = (a * a).astype(_BF16)
            return task

        def down():
            y = x_ref[rows, :] + _dot(hid_s[rows, :], wdn_ref[...])
            if final:
                y = _rms_scale(y, gfin_ref[...])
            o_ref[rows, :] = y

        return norm, [up(n) for n in range(n_col)] + [down]

    groups = [group_tasks(j) for j in range(n_sub)]
    groups[0][0]()
    for j, (_, matmuls) in enumerate(groups):
        for n, task in enumerate(matmuls):
            task()
            if n == 0 and j + 1 < n_sub:
                groups[j + 1][0]()


def _log_gamma(hd):
    return math.log(1.0 - 2.0 ** (-5.0 - hd))


def _retention_constants(C):
    lg = jnp.asarray([_log_gamma(hd) for hd in range(RET_HEADS)], _F32)
    idx = jnp.arange(C, dtype=_F32)
    diff = idx[:, None] - idx[None, :]
    decay = jnp.where(diff >= 0, jnp.exp(lg[:, None, None] * jnp.maximum(diff, 0.0)), 0.0)
    cross = jnp.exp(lg[:, None] * (idx[None, :] + 1.0))
    kdec = jnp.exp(lg[:, None] * (C - 1.0 - idx[None, :]))
    bcast = lambda a: jnp.broadcast_to(a[:, :, None], (RET_HEADS, C, HEAD_DIM))
    return decay * Q_SCALE, bcast(cross * Q_SCALE), bcast(kdec)


def _rope_tables(pos):
    half = HEAD_DIM // 2
    inv_freq = ROPE_BASE ** (-jnp.arange(half, dtype=_F32) / half)
    ang = pos[:, None] * inv_freq[None, :]
    cos, sin = jnp.cos(ang), jnp.sin(ang)
    return jnp.concatenate([cos, cos], axis=-1), jnp.concatenate([-sin, sin], axis=-1)


def _const_spec(shape):
    return pl.BlockSpec(shape, lambda *_: (0,) * len(shape), pipeline_mode=pl.Buffered(1))


def _layer_spec(shape, layer):
    return pl.BlockSpec((None,) + shape, lambda *_: (layer, 0, 0), pipeline_mode=pl.Buffered(1))


def _mixer(x2d, s0, acc, lw, tables, *, layer, n_streams, S, nb, T, C, group_rows):
    has_state = s0 is not None
    want_gv = len(acc) == 2
    assert S % T == 0 and T % C == 0 and n_streams % nb == 0
    assert C == min(S, GM_CHUNK)
    assert nb == 1 or S == T
    R = nb * T
    n_t = S // T
    cos, sin, decay, cross, kdec = tables
    gmw = lw['gm_w'][:, :C, :C]
    gmb = jnp.broadcast_to(lw['gm_b'][:, :C, None], (GM_GROUPS, C, HEAD_DIM))

    grid = (n_streams // nb, n_t)
    row_spec = lambda width: pl.BlockSpec((R, width), lambda b, t: (b * n_t + t, 0))
    state_spec = pl.BlockSpec((None, nb, RET_HEADS, HEAD_DIM, HEAD_DIM),
                              lambda b, t: (layer, b, 0, 0, 0))
    rope_spec = pl.BlockSpec((R, HEAD_DIM), lambda b, t: (t, 0))
    in_specs = [row_spec(D_MODEL)]
    args = [x2d]
    if has_state:
        in_specs.append(state_spec)
        args.append(s0)
    in_specs += [
        _const_spec((1, D_MODEL)),
        _layer_spec((D_MODEL, 6 * RET_WIDTH), layer),
        rope_spec,
        rope_spec,
        _const_spec((RET_HEADS, C, C)),
        _const_spec((RET_HEADS, C, HEAD_DIM)),
        _const_spec((RET_HEADS, C, HEAD_DIM)),
        _const_spec((1, RET_WIDTH)),
        _const_spec((1, GM_WIDTH)),
        _const_spec((1, GM_WIDTH)),
        _const_spec((GM_GROUPS, C, C)),
        _const_spec((GM_GROUPS, C, HEAD_DIM)),
        _layer_spec((D_MODEL, D_MODEL), layer),
    ]
    args += [lw['g_mix'], lw['w_in'], cos, sin, decay, cross, kdec, lw['ret_gn'],
             lw['gm_ln_g'], lw['gm_ln_b'], gmw, gmb, lw['w_out']]
    first_acc = len(args)
    in_specs += [pl.BlockSpec(memory_space=pl.ANY)] * len(acc)
    args += list(acc)

    out_shape = [jax.ShapeDtypeStruct(x2d.shape, _F32)]
    out_shape += [jax.ShapeDtypeStruct(a.shape, _F32) for a in acc]
    out_specs = [row_spec(D_MODEL), state_spec]
    if want_gv:
        out_specs.append(pl.BlockSpec((None, R, GM_WIDTH), lambda b, t: (layer, b * n_t + t, 0)))

    scratch = [pltpu.VMEM((R, RET_WIDTH), _BF16),
               pltpu.VMEM((R, RET_WIDTH), _BF16),
               pltpu.VMEM((R, RET_WIDTH), _BF16),
               pltpu.VMEM((R, RET_WIDTH), _F32),
               pltpu.VMEM((R, GM_WIDTH), _F32),
               pltpu.VMEM((R, GM_WIDTH), _BF16),
               pltpu.VMEM((R, D_MODEL), _BF16)]

    kern = functools.partial(_mixer_kernel, nb=nb, T=T, C=C, group_rows=group_rows,
                             has_state=has_state, want_gv=want_gv)
    return pl.pallas_call(
        kern,
        grid=grid,
        in_specs=in_specs,
        out_specs=out_specs,
        out_shape=out_shape,
        scratch_shapes=scratch,
        input_output_aliases={first_acc + n: 1 + n for n in range(len(acc))},
        compiler_params=pltpu.CompilerParams(
            dimension_semantics=("arbitrary",) * len(grid),
            vmem_limit_bytes=_VMEM_LIMIT_BYTES),
        name="mixer_state" if has_state else "mixer",
    )(*args)


def _ffn(x2d, lw, g_final, *, layer, R, final):
    n_rows = x2d.shape[0]
    assert n_rows % R == 0
    n_sub = max(1, R // _FFN_GROUP_ROWS)
    const = lambda shape: pl.BlockSpec(shape, lambda r: (0,) * len(shape),
                                       pipeline_mode=pl.Buffered(1))
    row_spec = pl.BlockSpec((R, D_MODEL), lambda r: (r, 0))
    return pl.pallas_call(
        functools.partial(_ffn_kernel, n_sub=n_sub, final=final),
        grid=(n_rows // R,),
        in_specs=[row_spec, const((1, D_MODEL)), _layer_spec((D_MODEL, D_FF), layer),
                  _layer_spec((D_FF, D_MODEL), layer), const((1, D_MODEL))],
        out_specs=row_spec,
        out_shape=jax.ShapeDtypeStruct(x2d.shape, _F32),
        scratch_shapes=[pltpu.VMEM((R, D_FF), _BF16)],
        compiler_params=pltpu.CompilerParams(
            dimension_semantics=("arbitrary",),
            vmem_limit_bytes=_VMEM_LIMIT_BYTES),
        name="ffn_final" if final else "ffn",
    )(x2d, lw['g_ffn'], lw['w_up'], lw['w_down'], g_final)


def _pick_tile(S, pref):
    t = min(S, pref)
    while S % t:
        t //= 2
    return t


def _forward(x_prompt, x_sample, state_ret, g_mix, w_in, ret_gn, gm_ln_g, gm_ln_b, gm_w, gm_b,
             w_out, g_ffn, w_up, w_down, g_final):
    B, S, D = x_prompt.shape
    DB, DS, _ = x_sample.shape
    depth = w_in.shape[0]

    T_p = _pick_tile(S, _PROMPT_TILE_ROWS)
    C_p = min(S, GM_CHUNK)
    groups_p = _PROMPT_GROUP_ROWS if sum(_PROMPT_GROUP_ROWS) == T_p else (T_p,)
    nb_s = math.gcd(DB, _SAMPLE_STREAMS_PER_TILE)
    groups_s = _SAMPLE_GROUP_ROWS if sum(_SAMPLE_GROUP_ROWS) == nb_s * DS else (nb_s * DS,)

    cos_p, sin_p = _rope_tables(jnp.arange(S, dtype=_F32))
    cos_s, sin_s = _rope_tables(PAST_LEN + jnp.arange(DS, dtype=_F32))
    cos_s, sin_s = jnp.tile(cos_s, (nb_s, 1)), jnp.tile(sin_s, (nb_s, 1))
    tab_p = (cos_p, sin_p) + _retention_constants(C_p)
    tab_s = (cos_s, sin_s) + _retention_constants(DS)

    row = lambda a: a.reshape(1, -1)
    g_fin = row(g_final)
    hp = x_prompt.reshape(B * S, D)
    hs = x_sample.reshape(DB * DS, D)
    R_ffn_p = _pick_tile(B * S, _FFN_TILE_ROWS)
    R_ffn_s = _pick_tile(DB * DS, _FFN_TILE_ROWS)

    w_in_b, w_out_b = w_in.astype(_BF16), w_out.astype(_BF16)
    w_up_b, w_down_b = w_up.astype(_BF16), w_down.astype(_BF16)

    state_p = jnp.zeros((depth, B, RET_HEADS, HEAD_DIM, HEAD_DIM), _F32)
    state_s = jnp.zeros((depth, DB, RET_HEADS, HEAD_DIM, HEAD_DIM), _F32)
    gv_s = jnp.zeros((depth, DB * DS, GM_WIDTH), _F32)
    for l in range(depth):
        lw = dict(g_mix=row(g_mix[l]), w_in=w_in_b, ret_gn=row(ret_gn[l]),
                  gm_ln_g=row(gm_ln_g[l]), gm_ln_b=row(gm_ln_b[l]), gm_w=gm_w[l], gm_b=gm_b[l],
                  w_out=w_out_b, g_ffn=row(g_ffn[l]), w_up=w_up_b, w_down=w_down_b)
        final = l == depth - 1
        hp, state_p = _mixer(hp, None, (state_p,), lw, tab_p, layer=l, n_streams=B, S=S, nb=1,
                             T=T_p, C=C_p, group_rows=groups_p)
        hp = _ffn(hp, lw, g_fin, layer=l, R=R_ffn_p, final=final)
        hs, state_s, gv_s = _mixer(hs, state_ret, (state_s, gv_s), lw, tab_s, layer=l,
                                   n_streams=DB, S=DS, nb=nb_s, T=DS, C=DS,
                                   group_rows=groups_s)
        hs = _ffn(hs, lw, g_fin, layer=l, R=R_ffn_s, final=final)

    return (hp.reshape(B, S, D), hs.reshape(DB, DS, D), state_p, state_s,
            gv_s.reshape(depth, DB, DS, GM_WIDTH))


def kernel(x_prompt, x_sample, state_ret, g_mix, w_in, ret_gn, gm_ln_g, gm_ln_b, gm_w, gm_b,
           w_out, g_ffn, w_up, w_down, g_final):
    return _forward(x_prompt, x_sample, state_ret, g_mix, w_in, ret_gn, gm_ln_g, gm_ln_b, gm_w,
                    gm_b, w_out, g_ffn, w_up, w_down, g_final)
```

```python
import functools
import math

import jax
import jax.numpy as jnp
from jax import lax
from jax.experimental import pallas as pl
from jax.experimental.pallas import tpu as pltpu

D_MODEL = 1024
PAST_LEN = 4096
RET_HEADS = 4
HEAD_DIM = 128
RET_WIDTH = RET_HEADS * HEAD_DIM
GM_GROUPS = 4
GM_WIDTH = GM_GROUPS * HEAD_DIM
GM_CHUNK = 128
D_FF = 4 * D_MODEL
ROPE_BASE = 10000.0
EPS = 1e-6
Q_SCALE = HEAD_DIM ** -0.5

_OFF_Q, _OFF_K, _OFF_V, _OFF_GATE, _OFF_U, _OFF_GV = (i * RET_WIDTH for i in range(6))

_VMEM_LIMIT_BYTES = 56 * 1024 * 1024
_FFN_COL_CHUNK = 1024
_FFN_TILE_ROWS = 1024
_FFN_GROUP_ROWS = 512
_PROMPT_TILE_ROWS = 1024
_PROMPT_GROUP_ROWS = (512, 512)
_SAMPLE_GROUP_ROWS = (256, 256)
_SAMPLE_STREAMS_PER_TILE = 16
_SMALL_TASK_BATCH = 4

_BF16 = jnp.bfloat16
_F32 = jnp.float32


def _rms_scale(x, g):
    ms = jnp.mean(x * x, axis=-1, keepdims=True)
    return x * lax.rsqrt(ms + EPS) * g


def _gelu_tanh(x):
    c = math.sqrt(2.0 / math.pi)
    return 0.5 * x * (1.0 + jnp.tanh(c * (x + 0.044715 * (x * x * x))))


def _dot(a, b):
    return jnp.dot(a, b, preferred_element_type=_F32)


def _interleave(first, second):
    if not second:
        return list(first)
    if not first:
        return list(second)
    out, j = [], 0
    for i, task in enumerate(first):
        out.append(task)
        want = ((i + 1) * len(second)) // len(first)
        while j < want:
            out.append(second[j])
            j += 1
    return out


def _mixer_kernel(*refs, nb, T, C, group_rows, has_state, want_gv):
    it = iter(refs)
    x_ref = next(it)
    s0_ref = next(it) if has_state else None
    (gmix_ref, win_ref, cos_ref, sin_ref, decay_ref, cross_ref, kdec_ref, retgn_ref,
     lng_ref, lnb_ref, gmw_ref, gmb_ref, wout_ref) = (next(it) for _ in range(13))
    for _ in range(2 if want_gv else 1):
        next(it)
    xo_ref = next(it)
    so_ref = next(it)
    gv_ref = next(it) if want_gv else None
    q_s, k_s, v_s, gate_s, u_s, gvb_s, mix_s = (next(it) for _ in range(7))

    R = nb * T
    n_sub = len(group_rows)
    bounds = [sum(group_rows[:j]) for j in range(n_sub + 1)]
    assert bounds[-1] == R and all(rows % C == 0 for rows in group_rows)

    @pl.when(pl.program_id(1) == 0)
    def _():
        if has_state:
            so_ref[...] = s0_ref[...]
        else:
            so_ref[...] = jnp.zeros_like(so_ref)

    row_id = lax.broadcasted_iota(jnp.int32, (C, C), 0)
    col_id = lax.broadcasted_iota(jnp.int32, (C, C), 1)
    causal = row_id >= col_id
    ws = [jnp.where(causal, gmw_ref[g], 0.0).astype(_BF16) for g in range(GM_GROUPS)]

    def projection_tasks(j):
        rows = slice(bounds[j], bounds[j + 1])
        val = {}

        def norm():
            val['h'] = _rms_scale(x_ref[rows, :], gmix_ref[...]).astype(_BF16)

        def proj(name, off):
            def task():
                val[name] = _dot(val['h'], win_ref[:, off:off + RET_WIDTH])
            return task

        def rope_to(dst, name):
            def task():
                cos = cos_ref[rows, :]
                sin = sin_ref[rows, :]
                p = val.pop(name)
                for hd in range(RET_HEADS):
                    sl = slice(hd * HEAD_DIM, (hd + 1) * HEAD_DIM)
                    ph = p[:, sl]
                    dst[rows, sl] = (ph * cos + pltpu.roll(ph, HEAD_DIM // 2, 1) * sin).astype(_BF16)
            return task

        def u_act():
            u_s[rows, :] = _gelu_tanh(val.pop('u'))

        def gv_act():
            gvp = _gelu_tanh(val.pop('gv'))
            mu = jnp.mean(gvp, axis=-1, keepdims=True)
            gc = gvp - mu
            gv = gc * lax.rsqrt(jnp.mean(gc * gc, axis=-1, keepdims=True) + EPS)
            gv = gv * lng_ref[...] + lnb_ref[...]
            if want_gv:
                gv_ref[rows, :] = gv
            gvb_s[rows, :] = gv.astype(_BF16)

        def gate_act():
            pgate = val.pop('gate')
            gate_s[rows, :] = pgate * jax.nn.sigmoid(pgate) * retgn_ref[...]

        def v_act():
            v_s[rows, :] = val.pop('v').astype(_BF16)

        return [[norm, proj('k', _OFF_K)],
                [proj('q', _OFF_Q)],
                [proj('v', _OFF_V), rope_to(k_s, 'k')],
                [proj('gate', _OFF_GATE), rope_to(q_s, 'q')],
                [proj('u', _OFF_U), v_act],
                [proj('gv', _OFF_GV), gate_act, u_act, gv_act]]

    def retention_scores(stream, r0, hd, val):
        rows = slice(r0, r0 + C)
        sl = slice(hd * HEAD_DIM, (hd + 1) * HEAD_DIM)
        qh = q_s[rows, sl]
        kh = k_s[rows, sl]
        state = so_ref[stream, hd]
        val['sc'] = lax.dot_general(qh, kh, (((1,), (1,)), ((), ())),
                                    preferred_element_type=_F32)
        val['cross'] = _dot(qh, state.astype(_BF16))
        kd = (kh.astype(_F32) * kdec_ref[hd]).astype(_BF16)
        kv = lax.dot_general(kd, v_s[rows, sl], (((0,), (0,)), ((), ())),
                             preferred_element_type=_F32)
        so_ref[stream, hd] = state * math.exp(_log_gamma(hd) * C) + kv

    def retention_output(r0, hd, val):
        rows = slice(r0, r0 + C)
        sl = slice(hd * HEAD_DIM, (hd + 1) * HEAD_DIM)
        sc = (val.pop('sc') * decay_ref[hd]).astype(_BF16)
        o = _dot(sc, v_s[rows, sl]) + val.pop('cross') * cross_ref[hd]
        on = o * lax.rsqrt(jnp.mean(o * o, axis=-1, keepdims=True) + EPS)
        mix_s[rows, sl] = (gate_s[rows, sl] * on).astype(_BF16)

    def gmlp_block(r0, g):
        rows = slice(r0, r0 + C)
        sl = slice(g * HEAD_DIM, (g + 1) * HEAD_DIM)
        osl = slice(RET_WIDTH + g * HEAD_DIM, RET_WIDTH + (g + 1) * HEAD_DIM)
        z = _dot(ws[g], gvb_s[rows, sl]) + gmb_ref[g]
        mix_s[rows, osl] = (u_s[rows, sl] * z).astype(_BF16)

    def chunk_items(j):
        items = []
        for r0 in range(bounds[j], bounds[j + 1], C):
            for hd in range(RET_HEADS):
                val = {}
                items.append((functools.partial(retention_scores, r0 // T, r0, hd, val),
                              [functools.partial(gmlp_block, r0, hd),
                               functools.partial(retention_output, r0, hd, val)]))
        return items

    def output_tasks(j):
        rows = slice(bounds[j], bounds[j + 1])
        val = {}

        def first():
            val['y'] = _dot(mix_s[rows, :RET_WIDTH], wout_ref[:RET_WIDTH, :])

        def second():
            y = val.pop('y') + _dot(mix_s[rows, RET_WIDTH:], wout_ref[RET_WIDTH:, :])
            xo_ref[rows, :] = x_ref[rows, :] + y

        return [[first], [second]]

    def phase(bigs, items):
        tasks = []
        n_slots = max(len(bigs), -(-len(items) // _SMALL_TASK_BATCH))
        per = -(-len(items) // n_slots) if items else 0
        pending = []
        for slot in range(n_slots):
            if slot < len(bigs):
                tasks += bigs[slot]
            batch = items[slot * per:(slot + 1) * per]
            tasks += [scores for scores, _ in batch]
            tasks += pending
            pending = [task for _, later in batch for task in later]
        return tasks + pending

    program = []
    for p in range(n_sub + 2):
        bigs = _interleave(projection_tasks(p) if p < n_sub else [],
                           output_tasks(p - 2) if 0 <= p - 2 < n_sub else [])
        program += phase(bigs, chunk_items(p - 1) if 0 <= p - 1 < n_sub else [])
    for task in program:
        task()


def _ffn_kernel(x_ref, g_ref, wup_ref, wdn_ref, gfin_ref, o_ref, hid_s, *, n_sub, final):
    M = x_ref.shape[0] // n_sub
    n_col = D_FF // _FFN_COL_CHUNK

    def group_tasks(j):
        rows = slice(j * M, (j + 1) * M)
        val = {}

        def norm():
            val['h'] = _rms_scale(x_ref[rows, :], g_ref[...]).astype(_BF16)

        def up(n):
            def task():
                cols = slice(n * _FFN_COL_CHUNK, (n + 1) * _FFN_COL_CHUNK)
                a = jnp.maximum(_dot(val['h'], wup_ref[:, cols]), 0.0)
                hid_s[rows, cols] = (a * a).astype(_BF16)
            return task

        def down():
            y = x_ref[rows, :] + _dot(hid_s[rows, :], wdn_ref[...])
            if final:
                y = _rms_scale(y, gfin_ref[...])
            o_ref[rows, :] = y

        return norm, [up(n) for n in range(n_col)] + [down]

    groups = [group_tasks(j) for j in range(n_sub)]
    groups[0][0]()
    for j, (_, matmuls) in enumerate(groups):
        for n, task in enumerate(matmuls):
            task()
            if n == 0 and j + 1 < n_sub:
                groups[j + 1][0]()


def _log_gamma(hd):
    return math.log(1.0 - 2.0 ** (-5.0 - hd))


def _retention_constants(C):
    lg = jnp.asarray([_log_gamma(hd) for hd in range(RET_HEADS)], _F32)
    idx = jnp.arange(C, dtype=_F32)
    diff = idx[:, None] - idx[None, :]
    decay = jnp.where(diff >= 0, jnp.exp(lg[:, None, None] * jnp.maximum(diff, 0.0)), 0.0)
    cross = jnp.exp(lg[:, None] * (idx[None, :] + 1.0))
    kdec = jnp.exp(lg[:, None] * (C - 1.0 - idx[None, :]))
    bcast = lambda a: jnp.broadcast_to(a[:, :, None], (RET_HEADS, C, HEAD_DIM))
    return decay * Q_SCALE, bcast(cross * Q_SCALE), bcast(kdec)


def _rope_tables(pos):
    half = HEAD_DIM // 2
    inv_freq = ROPE_BASE ** (-jnp.arange(half, dtype=_F32) / half)
    ang = pos[:, None] * inv_freq[None, :]
    cos, sin = jnp.cos(ang), jnp.sin(ang)
    return jnp.concatenate([cos, cos], axis=-1), jnp.concatenate([-sin, sin], axis=-1)


def _const_spec(shape):
    return pl.BlockSpec(shape, lambda *_: (0,) * len(shape), pipeline_mode=pl.Buffered(1))


def _layer_spec(shape, layer):
    return pl.BlockSpec((None,) + shape, lambda *_: (layer, 0, 0), pipeline_mode=pl.Buffered(1))


def _mixer(x2d, s0, acc, lw, tables, *, layer, n_streams, S, nb, T, C, group_rows):
    has_state = s0 is not None
    want_gv = len(acc) == 2
    assert S % T == 0 and T % C == 0 and n_streams % nb == 0
    assert C == min(S, GM_CHUNK)
    assert nb == 1 or S == T
    R = nb * T
    n_t = S // T
    cos, sin, decay, cross, kdec = tables
    gmw = lw['gm_w'][:, :C, :C]
    gmb = jnp.broadcast_to(lw['gm_b'][:, :C, None], (GM_GROUPS, C, HEAD_DIM))

    grid = (n_streams // nb, n_t)
    row_spec = lambda width: pl.BlockSpec((R, width), lambda b, t: (b * n_t + t, 0))
    state_spec = pl.BlockSpec((None, nb, RET_HEADS, HEAD_DIM, HEAD_DIM),
                              lambda b, t: (layer, b, 0, 0, 0))
    rope_spec = pl.BlockSpec((R, HEAD_DIM), lambda b, t: (t, 0))
    in_specs = [row_spec(D_MODEL)]
    args = [x2d]
    if has_state:
        in_specs.append(state_spec)
        args.append(s0)
    in_specs += [
        _const_spec((1, D_MODEL)),
        _layer_spec((D_MODEL, 6 * RET_WIDTH), layer),
        rope_spec,
        rope_spec,
        _const_spec((RET_HEADS, C, C)),
        _const_spec((RET_HEADS, C, HEAD_DIM)),
        _const_spec((RET_HEADS, C, HEAD_DIM)),
        _const_spec((1, RET_WIDTH)),
        _const_spec((1, GM_WIDTH)),
        _const_spec((1, GM_WIDTH)),
        _const_spec((GM_GROUPS, C, C)),
        _const_spec((GM_GROUPS, C, HEAD_DIM)),
        _layer_spec((D_MODEL, D_MODEL), layer),
    ]
    args += [lw['g_mix'], lw['w_in'], cos, sin, decay, cross, kdec, lw['ret_gn'],
             lw['gm_ln_g'], lw['gm_ln_b'], gmw, gmb, lw['w_out']]
    first_acc = len(args)
    in_specs += [pl.BlockSpec(memory_space=pl.ANY)] * len(acc)
    args += list(acc)

    out_shape = [jax.ShapeDtypeStruct(x2d.shape, _F32)]
    out_shape += [jax.ShapeDtypeStruct(a.shape, _F32) for a in acc]
    out_specs = [row_spec(D_MODEL), state_spec]
    if want_gv:
        out_specs.append(pl.BlockSpec((None, R, GM_WIDTH), lambda b, t: (layer, b * n_t + t, 0)))

    scratch = [pltpu.VMEM((R, RET_WIDTH), _BF16),
               pltpu.VMEM((R, RET_WIDTH), _BF16),
               pltpu.VMEM((R, RET_WIDTH), _BF16),
               pltpu.VMEM((R, RET_WIDTH), _F32),
               pltpu.VMEM((R, GM_WIDTH), _F32),
               pltpu.VMEM((R, GM_WIDTH), _BF16),
               pltpu.VMEM((R, D_MODEL), _BF16)]

    kern = functools.partial(_mixer_kernel, nb=nb, T=T, C=C, group_rows=group_rows,
                             has_state=has_state, want_gv=want_gv)
    return pl.pallas_call(
        kern,
        grid=grid,
        in_specs=in_specs,
        out_specs=out_specs,
        out_shape=out_shape,
        scratch_shapes=scratch,
        input_output_aliases={first_acc + n: 1 + n for n in range(len(acc))},
        compiler_params=pltpu.CompilerParams(
            dimension_semantics=("arbitrary",) * len(grid),
            vmem_limit_bytes=_VMEM_LIMIT_BYTES),
        name="mixer_state" if has_state else "mixer",
    )(*args)


def _ffn(x2d, lw, g_final, *, layer, R, final):
    n_rows = x2d.shape[0]
    assert n_rows % R == 0
    n_sub = max(1, R // _FFN_GROUP_ROWS)
    const = lambda shape: pl.BlockSpec(shape, lambda r: (0,) * len(shape),
                                       pipeline_mode=pl.Buffered(1))
    row_spec = pl.BlockSpec((R, D_MODEL), lambda r: (r, 0))
    return pl.pallas_call(
        functools.partial(_ffn_kernel, n_sub=n_sub, final=final),
        grid=(n_rows // R,),
        in_specs=[row_spec, const((1, D_MODEL)), _layer_spec((D_MODEL, D_FF), layer),
                  _layer_spec((D_FF, D_MODEL), layer), const((1, D_MODEL))],
        out_specs=row_spec,
        out_shape=jax.ShapeDtypeStruct(x2d.shape, _F32),
        scratch_shapes=[pltpu.VMEM((R, D_FF), _BF16)],
        compiler_params=pltpu.CompilerParams(
            dimension_semantics=("arbitrary",),
            vmem_limit_bytes=_VMEM_LIMIT_BYTES),
        name="ffn_final" if final else "ffn",
    )(x2d, lw['g_ffn'], lw['w_up'], lw['w_down'], g_final)


def _pick_tile(S, pref):
    t = min(S, pref)
    while S % t:
        t //= 2
    return t


def _forward(x_prompt, x_sample, state_ret, g_mix, w_in, ret_gn, gm_ln_g, gm_ln_b, gm_w, gm_b,
             w_out, g_ffn, w_up, w_down, g_final):
    B, S, D = x_prompt.shape
    DB, DS, _ = x_sample.shape
    depth = w_in.shape[0]

    T_p = _pick_tile(S, _PROMPT_TILE_ROWS)
    C_p = min(S, GM_CHUNK)
    groups_p = _PROMPT_GROUP_ROWS if sum(_PROMPT_GROUP_ROWS) == T_p else (T_p,)
    nb_s = math.gcd(DB, _SAMPLE_STREAMS_PER_TILE)
    groups_s = _SAMPLE_GROUP_ROWS if sum(_SAMPLE_GROUP_ROWS) == nb_s * DS else (nb_s * DS,)

    cos_p, sin_p = _rope_tables(jnp.arange(S, dtype=_F32))
    cos_s, sin_s = _rope_tables(PAST_LEN + jnp.arange(DS, dtype=_F32))
    cos_s, sin_s = jnp.tile(cos_s, (nb_s, 1)), jnp.tile(sin_s, (nb_s, 1))
    tab_p = (cos_p, sin_p) + _retention_constants(C_p)
    tab_s = (cos_s, sin_s) + _retention_constants(DS)

    row = lambda a: a.reshape(1, -1)
    g_fin = row(g_final)
    hp = x_prompt.reshape(B * S, D)
    hs = x_sample.reshape(DB * DS, D)
    R_ffn_p = _pick_tile(B * S, _FFN_TILE_ROWS)
    R_ffn_s = _pick_tile(DB * DS, _FFN_GROUP_ROWS)

    w_in_b, w_out_b = w_in.astype(_BF16), w_out.astype(_BF16)
    w_up_b, w_down_b = w_up.astype(_BF16), w_down.astype(_BF16)

    state_p = jnp.zeros((depth, B, RET_HEADS, HEAD_DIM, HEAD_DIM), _F32)
    state_s = jnp.zeros((depth, DB, RET_HEADS, HEAD_DIM, HEAD_DIM), _F32)
    gv_s = jnp.zeros((depth, DB * DS, GM_WIDTH), _F32)
    for l in range(depth):
        lw = dict(g_mix=row(g_mix[l]), w_in=w_in_b, ret_gn=row(ret_gn[l]),
                  gm_ln_g=row(gm_ln_g[l]), gm_ln_b=row(gm_ln_b[l]), gm_w=gm_w[l], gm_b=gm_b[l],
                  w_out=w_out_b, g_ffn=row(g_ffn[l]), w_up=w_up_b, w_down=w_down_b)
        final = l == depth - 1
        hp, state_p = _mixer(hp, None, (state_p,), lw, tab_p, layer=l, n_streams=B, S=S, nb=1,
                             T=T_p, C=C_p, group_rows=groups_p)
        hp = _ffn(hp, lw, g_fin, layer=l, R=R_ffn_p, final=final)
        hs, state_s, gv_s = _mixer(hs, state_ret, (state_s, gv_s), lw, tab_s, layer=l,
                                   n_streams=DB, S=DS, nb=nb_s, T=DS, C=DS,
                                   group_rows=groups_s)
        hs = _ffn(hs, lw, g_fin, layer=l, R=R_ffn_s, final=final)

    return (hp.reshape(B, S, D), hs.reshape(DB, DS, D), state_p, state_s,
            gv_s.reshape(depth, DB, DS, GM_WIDTH))


def kernel(x_prompt, x_sample, state_ret, g_mix, w_in, ret_gn, gm_ln_g, gm_ln_b, gm_w, gm_b,
           w_out, g_ffn, w_up, w_down, g_final):
    return _forward(x_prompt, x_sample, state_ret, g_mix, w_in, ret_gn, gm_ln_g, gm_ln_b, gm_w,
                    gm_b, w_out, g_ffn, w_up, w_down, g_final)
```

```python
import functools
import math

import jax
import jax.numpy as jnp
from jax import lax
from jax.experimental import pallas as pl
from jax.experimental.pallas import tpu as pltpu

D_MODEL = 1024
PAST_LEN = 4096
RET_HEADS = 4
HEAD_DIM = 128
RET_WIDTH = RET_HEADS * HEAD_DIM
GM_GROUPS = 4
GM_WIDTH = GM_GROUPS * HEAD_DIM
GM_CHUNK = 128
D_FF = 4 * D_MODEL
ROPE_BASE = 10000.0
EPS = 1e-6
Q_SCALE = HEAD_DIM ** -0.5

_OFF_Q, _OFF_K, _OFF_V, _OFF_GATE, _OFF_U, _OFF_GV = (i * RET_WIDTH for i in range(6))

_VMEM_LIMIT_BYTES = 56 * 1024 * 1024
_FFN_COL_CHUNK = 1024
_FFN_TILE_ROWS = 1024
_FFN_GROUP_ROWS = 512
_PROMPT_TILE_ROWS = 1024
_PROMPT_GROUP_ROWS = (512, 512)
_SAMPLE_GROUP_ROWS = (512, 512)
_SAMPLE_STREAMS_PER_TILE = 32
_SMALL_TASK_BATCH = 4

_BF16 = jnp.bfloat16
_F32 = jnp.float32


def _rms_scale(x, g):
    ms = jnp.mean(x * x, axis=-1, keepdims=True)
    return x * lax.rsqrt(ms + EPS) * g


def _gelu_tanh(x):
    c = math.sqrt(2.0 / math.pi)
    return 0.5 * x * (1.0 + jnp.tanh(c * (x + 0.044715 * (x * x * x))))


def _dot(a, b):
    return jnp.dot(a, b, preferred_element_type=_F32)


def _interleave(first, second):
    if not second:
        return list(first)
    if not first:
        return list(second)
    out, j = [], 0
    for i, task in enumerate(first):
        out.append(task)
        want = ((i + 1) * len(second)) // len(first)
        while j < want:
            out.append(second[j])
            j += 1
    return out


def _mixer_kernel(*refs, nb, T, C, group_rows, has_state, want_gv):
    it = iter(refs)
    x_ref = next(it)
    s0_ref = next(it) if has_state else None
    (gmix_ref, win_ref, cos_ref, sin_ref, decay_ref, cross_ref, kdec_ref, retgn_ref,
     lng_ref, lnb_ref, gmw_ref, gmb_ref, wout_ref) = (next(it) for _ in range(13))
    for _ in range(2 if want_gv else 1):
        next(it)
    xo_ref = next(it)
    so_ref = next(it)
    gv_ref = next(it) if want_gv else None
    q_s, k_s, v_s, gate_s, u_s, gvb_s, mix_s = (next(it) for _ in range(7))

    R = nb * T
    n_sub = len(group_rows)
    bounds = [sum(group_rows[:j]) for j in range(n_sub + 1)]
    assert bounds[-1] == R and all(rows % C == 0 for rows in group_rows)

    @pl.when(pl.program_id(1) == 0)
    def _():
        if has_state:
            so_ref[...] = s0_ref[...]
        else:
            so_ref[...] = jnp.zeros_like(so_ref)

    row_id = lax.broadcasted_iota(jnp.int32, (C, C), 0)
    col_id = lax.broadcasted_iota(jnp.int32, (C, C), 1)
    causal = row_id >= col_id
    ws = [jnp.where(causal, gmw_ref[g], 0.0).astype(_BF16) for g in range(GM_GROUPS)]

    def projection_tasks(j):
        rows = slice(bounds[j], bounds[j + 1])
        val = {}

        def norm():
            val['h'] = _rms_scale(x_ref[rows, :], gmix_ref[...]).astype(_BF16)

        def proj(name, off):
            def task():
                val[name] = _dot(val['h'], win_ref[:, off:off + RET_WIDTH])
            return task

        def rope_to(dst, name):
            def task():
                cos = cos_ref[rows, :]
                sin = sin_ref[rows, :]
                p = val.pop(name)
                for hd in range(RET_HEADS):
                    sl = slice(hd * HEAD_DIM, (hd + 1) * HEAD_DIM)
                    ph = p[:, sl]
                    dst[rows, sl] = (ph * cos + pltpu.roll(ph, HEAD_DIM // 2, 1) * sin).astype(_BF16)
            return task

        def u_act():
            u_s[rows, :] = _gelu_tanh(val.pop('u'))

        def gv_act():
            gvp = _gelu_tanh(val.pop('gv'))
            mu = jnp.mean(gvp, axis=-1, keepdims=True)
            gc = gvp - mu
            gv = gc * lax.rsqrt(jnp.mean(gc * gc, axis=-1, keepdims=True) + EPS)
            gv = gv * lng_ref[...] + lnb_ref[...]
            if want_gv:
                gv_ref[rows, :] = gv
            gvb_s[rows, :] = gv.astype(_BF16)

        def gate_act():
            pgate = val.pop('gate')
            gate_s[rows, :] = pgate * jax.nn.sigmoid(pgate) * retgn_ref[...]

        def v_act():
            v_s[rows, :] = val.pop('v').astype(_BF16)

        return [[norm, proj('k', _OFF_K)],
                [proj('q', _OFF_Q)],
                [proj('v', _OFF_V), rope_to(k_s, 'k')],
                [proj('gate', _OFF_GATE), rope_to(q_s, 'q')],
                [proj('u', _OFF_U), v_act],
                [proj('gv', _OFF_GV), gate_act, u_act, gv_act]]

    def retention_scores(stream, r0, hd, val):
        rows = slice(r0, r0 + C)
        sl = slice(hd * HEAD_DIM, (hd + 1) * HEAD_DIM)
        qh = q_s[rows, sl]
        kh = k_s[rows, sl]
        state = so_ref[stream, hd]
        kt = kh.astype(_F32).T
        val['sc'] = _dot(qh, kt.astype(_BF16))
        val['cross'] = _dot(qh, state.astype(_BF16))
        kv = _dot((kt * kdec_ref[hd]).astype(_BF16), v_s[rows, sl])
        so_ref[stream, hd] = state * math.exp(_log_gamma(hd) * C) + kv

    def retention_output(r0, hd, val):
        rows = slice(r0, r0 + C)
        sl = slice(hd * HEAD_DIM, (hd + 1) * HEAD_DIM)
        sc = (val.pop('sc') * decay_ref[hd]).astype(_BF16)
        o = _dot(sc, v_s[rows, sl]) + val.pop('cross') * cross_ref[hd]
        on = o * lax.rsqrt(jnp.mean(o * o, axis=-1, keepdims=True) + EPS)
        mix_s[rows, sl] = (gate_s[rows, sl] * on).astype(_BF16)

    def gmlp_block(r0, g):
        rows = slice(r0, r0 + C)
        sl = slice(g * HEAD_DIM, (g + 1) * HEAD_DIM)
        osl = slice(RET_WIDTH + g * HEAD_DIM, RET_WIDTH + (g + 1) * HEAD_DIM)
        z = _dot(ws[g], gvb_s[rows, sl]) + gmb_ref[g]
        mix_s[rows, osl] = (u_s[rows, sl] * z).astype(_BF16)

    def chunk_items(j):
        items = []
        for r0 in range(bounds[j], bounds[j + 1], C):
            for hd in range(RET_HEADS):
                val = {}
                items.append((functools.partial(retention_scores, r0 // T, r0, hd, val),
                              [functools.partial(gmlp_block, r0, hd),
                               functools.partial(retention_output, r0, hd, val)]))
        return items

    def output_tasks(j):
        rows = slice(bounds[j], bounds[j + 1])
        val = {}

        def first():
            val['y'] = _dot(mix_s[rows, :RET_WIDTH], wout_ref[:RET_WIDTH, :])

        def second():
            y = val.pop('y') + _dot(mix_s[rows, RET_WIDTH:], wout_ref[RET_WIDTH:, :])
            xo_ref[rows, :] = x_ref[rows, :] + y

        return [[first], [second]]

    def phase(bigs, items):
        tasks = []
        n_slots = max(len(bigs), -(-len(items) // _SMALL_TASK_BATCH))
        per = -(-len(items) // n_slots) if items else 0
        pending = []
        for slot in range(n_slots):
            if slot < len(bigs):
                tasks += bigs[slot]
            batch = items[slot * per:(slot + 1) * per]
            tasks += [scores for scores, _ in batch]
            tasks += pending
            pending = [task for _, later in batch for task in later]
        return tasks + pending

    program = []
    for p in range(n_sub + 2):
        bigs = _interleave(projection_tasks(p) if p < n_sub else [],
                           output_tasks(p - 2) if 0 <= p - 2 < n_sub else [])
        program += phase(bigs, chunk_items(p - 1) if 0 <= p - 1 < n_sub else [])
    for task in program:
        task()


def _ffn_kernel(x_ref, g_ref, wup_ref, wdn_ref, gfin_ref, o_ref, hid_s, *, n_sub, final):
    M = x_ref.shape[0] // n_sub
    n_col = D_FF // _FFN_COL_CHUNK

    def group_tasks(j):
        rows = slice(j * M, (j + 1) * M)
        val = {}

        def norm():
            val['h'] = _rms_scale(x_ref[rows, :], g_ref[...]).astype(_BF16)

        def up(n):
            def task():
                cols = slice(n * _FFN_COL_CHUNK, (n + 1) * _FFN_COL_CHUNK)
                a = jnp.maximum(_dot(val['h'], wup_ref[:, cols]), 0.0)
                hid_s[rows, cols] = (a * a).astype(_BF16)
            return task

        def down():
            y = x_ref[rows, :] + _dot(hid_s[rows, :], wdn_ref[...])
            if final:
                y = _rms_scale(y, gfin_ref[...])
            o_ref[rows, :] = y

        return norm, [up(n) for n in range(n_col)] + [down]

    groups = [group_tasks(j) for j in range(n_sub)]
    groups[0][0]()
    for j, (_, matmuls) in enumerate(groups):
        for n, task in enumerate(matmuls):
            task()
            if n == 0 and j + 1 < n_sub:
                groups[j + 1][0]()


def _log_gamma(hd):
    return math.log(1.0 - 2.0 ** (-5.0 - hd))


def _retention_constants(C):
    lg = jnp.asarray([_log_gamma(hd) for hd in range(RET_HEADS)], _F32)
    idx = jnp.arange(C, dtype=_F32)
    diff = idx[:, None] - idx[None, :]
    decay = jnp.where(diff >= 0, jnp.exp(lg[:, None, None] * jnp.maximum(diff, 0.0)), 0.0)
    cross = jnp.exp(lg[:, None] * (idx[None, :] + 1.0))
    kdec = jnp.exp(lg[:, None] * (C - 1.0 - idx[None, :]))
    bcast = lambda a: jnp.broadcast_to(a[:, :, None], (RET_HEADS, C, HEAD_DIM))
    kdec_t = jnp.broadcast_to(kdec[:, None, :], (RET_HEADS, HEAD_DIM, C))
    return decay * Q_SCALE, bcast(cross * Q_SCALE), kdec_t


def _rope_tables(pos):
    half = HEAD_DIM // 2
    inv_freq = ROPE_BASE ** (-jnp.arange(half, dtype=_F32) / half)
    ang = pos[:, None] * inv_freq[None, :]
    cos, sin = jnp.cos(ang), jnp.sin(ang)
    return jnp.concatenate([cos, cos], axis=-1), jnp.concatenate([-sin, sin], axis=-1)


def _const_spec(shape):
    return pl.BlockSpec(shape, lambda *_: (0,) * len(shape), pipeline_mode=pl.Buffered(1))


def _layer_spec(shape, layer):
    return pl.BlockSpec((None,) + shape, lambda *_: (layer, 0, 0), pipeline_mode=pl.Buffered(1))


def _mixer(x2d, s0, acc, lw, tables, *, layer, n_streams, S, nb, T, C, group_rows):
    has_state = s0 is not None
    want_gv = len(acc) == 2
    assert S % T == 0 and T % C == 0 and n_streams % nb == 0
    assert C == min(S, GM_CHUNK)
    assert nb == 1 or S == T
    R = nb * T
    n_t = S // T
    cos, sin, decay, cross, kdec = tables
    gmw = lw['gm_w'][:, :C, :C]
    gmb = jnp.broadcast_to(lw['gm_b'][:, :C, None], (GM_GROUPS, C, HEAD_DIM))

    grid = (n_streams // nb, n_t)
    row_spec = lambda width: pl.BlockSpec((R, width), lambda b, t: (b * n_t + t, 0))
    state_spec = pl.BlockSpec((None, nb, RET_HEADS, HEAD_DIM, HEAD_DIM),
                              lambda b, t: (layer, b, 0, 0, 0))
    rope_spec = pl.BlockSpec((R, HEAD_DIM), lambda b, t: (t, 0))
    in_specs = [row_spec(D_MODEL)]
    args = [x2d]
    if has_state:
        in_specs.append(state_spec)
        args.append(s0)
    in_specs += [
        _const_spec((1, D_MODEL)),
        _layer_spec((D_MODEL, 6 * RET_WIDTH), layer),
        rope_spec,
        rope_spec,
        _const_spec((RET_HEADS, C, C)),
        _const_spec((RET_HEADS, C, HEAD_DIM)),
        _const_spec((RET_HEADS, HEAD_DIM, C)),
        _const_spec((1, RET_WIDTH)),
        _const_spec((1, GM_WIDTH)),
        _const_spec((1, GM_WIDTH)),
        _const_spec((GM_GROUPS, C, C)),
        _const_spec((GM_GROUPS, C, HEAD_DIM)),
        _layer_spec((D_MODEL, D_MODEL), layer),
    ]
    args += [lw['g_mix'], lw['w_in'], cos, sin, decay, cross, kdec, lw['ret_gn'],
             lw['gm_ln_g'], lw['gm_ln_b'], gmw, gmb, lw['w_out']]
    first_acc = len(args)
    in_specs += [pl.BlockSpec(memory_space=pl.ANY)] * len(acc)
    args += list(acc)

    out_shape = [jax.ShapeDtypeStruct(x2d.shape, _F32)]
    out_shape += [jax.ShapeDtypeStruct(a.shape, _F32) for a in acc]
    out_specs = [row_spec(D_MODEL), state_spec]
    if want_gv:
        out_specs.append(pl.BlockSpec((None, R, GM_WIDTH), lambda b, t: (layer, b * n_t + t, 0)))

    scratch = [pltpu.VMEM((R, RET_WIDTH), _BF16),
               pltpu.VMEM((R, RET_WIDTH), _BF16),
               pltpu.VMEM((R, RET_WIDTH), _BF16),
               pltpu.VMEM((R, RET_WIDTH), _F32),
               pltpu.VMEM((R, GM_WIDTH), _F32),
               pltpu.VMEM((R, GM_WIDTH), _BF16),
               pltpu.VMEM((R, D_MODEL), _BF16)]

    kern = functools.partial(_mixer_kernel, nb=nb, T=T, C=C, group_rows=group_rows,
                             has_state=has_state, want_gv=want_gv)
    return pl.pallas_call(
        kern,
        grid=grid,
        in_specs=in_specs,
        out_specs=out_specs,
        out_shape=out_shape,
        scratch_shapes=scratch,
        input_output_aliases={first_acc + n: 1 + n for n in range(len(acc))},
        compiler_params=pltpu.CompilerParams(
            dimension_semantics=("arbitrary",) * len(grid),
            vmem_limit_bytes=_VMEM_LIMIT_BYTES),
        name="mixer_state" if has_state else "mixer",
    )(*args)


def _ffn(x2d, lw, g_final, *, layer, R, final):
    n_rows = x2d.shape[0]
    assert n_rows % R == 0
    n_sub = max(1, R // _FFN_GROUP_ROWS)
    const = lambda shape: pl.BlockSpec(shape, lambda r: (0,) * len(shape),
                                       pipeline_mode=pl.Buffered(1))
    row_spec = pl.BlockSpec((R, D_MODEL), lambda r: (r, 0))
    return pl.pallas_call(
        functools.partial(_ffn_kernel, n_sub=n_sub, final=final),
        grid=(n_rows // R,),
        in_specs=[row_spec, const((1, D_MODEL)), _layer_spec((D_MODEL, D_FF), layer),
                  _layer_spec((D_FF, D_MODEL), layer), const((1, D_MODEL))],
        out_specs=row_spec,
        out_shape=jax.ShapeDtypeStruct(x2d.shape, _F32),
        scratch_shapes=[pltpu.VMEM((R, D_FF), _BF16)],
        compiler_params=pltpu.CompilerParams(
            dimension_semantics=("arbitrary",),
            vmem_limit_bytes=_VMEM_LIMIT_BYTES),
        name="ffn_final" if final else "ffn",
    )(x2d, lw['g_ffn'], lw['w_up'], lw['w_down'], g_final)


def _pick_tile(S, pref):
    t = min(S, pref)
    while S % t:
        t //= 2
    return t


def _forward(x_prompt, x_sample, state_ret, g_mix, w_in, ret_gn, gm_ln_g, gm_ln_b, gm_w, gm_b,
             w_out, g_ffn, w_up, w_down, g_final):
    B, S, D = x_prompt.shape
    DB, DS, _ = x_sample.shape
    depth = w_in.shape[0]

    T_p = _pick_tile(S, _PROMPT_TILE_ROWS)
    C_p = min(S, GM_CHUNK)
    groups_p = _PROMPT_GROUP_ROWS if sum(_PROMPT_GROUP_ROWS) == T_p else (T_p,)
    nb_s = math.gcd(DB, _SAMPLE_STREAMS_PER_TILE)
    groups_s = _SAMPLE_GROUP_ROWS if sum(_SAMPLE_GROUP_ROWS) == nb_s * DS else (nb_s * DS,)

    cos_p, sin_p = _rope_tables(jnp.arange(S, dtype=_F32))
    cos_s, sin_s = _rope_tables(PAST_LEN + jnp.arange(DS, dtype=_F32))
    cos_s, sin_s = jnp.tile(cos_s, (nb_s, 1)), jnp.tile(sin_s, (nb_s, 1))
    tab_p = (cos_p, sin_p) + _retention_constants(C_p)
    tab_s = (cos_s, sin_s) + _retention_constants(DS)

    row = lambda a: a.reshape(1, -1)
    g_fin = row(g_final)
    hp = x_prompt.reshape(B * S, D)
    hs = x_sample.reshape(DB * DS, D)
    R_ffn_p = _pick_tile(B * S, _FFN_TILE_ROWS)
    R_ffn_s = _pick_tile(DB * DS, _FFN_TILE_ROWS)

    w_in_b, w_out_b = w_in.astype(_BF16), w_out.astype(_BF16)
    w_up_b, w_down_b = w_up.astype(_BF16), w_down.astype(_BF16)

    state_p = jnp.zeros((depth, B, RET_HEADS, HEAD_DIM, HEAD_DIM), _F32)
    state_s = jnp.zeros((depth, DB, RET_HEADS, HEAD_DIM, HEAD_DIM), _F32)
    gv_s = jnp.zeros((depth, DB * DS, GM_WIDTH), _F32)
    for l in range(depth):
        lw = dict(g_mix=row(g_mix[l]), w_in=w_in_b, ret_gn=row(ret_gn[l]),
                  gm_ln_g=row(gm_ln_g[l]), gm_ln_b=row(gm_ln_b[l]), gm_w=gm_w[l], gm_b=gm_b[l],
                  w_out=w_out_b, g_ffn=row(g_ffn[l]), w_up=w_up_b, w_down=w_down_b)
        final = l == depth - 1
        hp, state_p = _mixer(hp, None, (state_p,), lw, tab_p, layer=l, n_streams=B, S=S, nb=1,
                             T=T_p, C=C_p, group_rows=groups_p)
        hp = _ffn(hp, lw, g_fin, layer=l, R=R_ffn_p, final=final)
        hs, state_s, gv_s = _mixer(hs, state_ret, (state_s, gv_s), lw, tab_s, layer=l,
                                   n_streams=DB, S=DS, nb=nb_s, T=DS, C=DS,
                                   group_rows=groups_s)
        hs = _ffn(hs, lw, g_fin, layer=l, R=R_ffn_s, final=final)

    return (hp.reshape(B, S, D), hs.reshape(DB, DS, D), state_p, state_s,
            gv_s.reshape(depth, DB, DS, GM_WIDTH))


def kernel(x_prompt, x_sample, state_ret, g_mix, w_in, ret_gn, gm_ln_g, gm_ln_b, gm_w, gm_b,
           w_out, g_ffn, w_up, w_down, g_final):
    return _forward(x_prompt, x_sample, state_ret, g_mix, w_in, ret_gn, gm_ln_g, gm_ln_b, gm_w,
                    gm_b, w_out, g_ffn, w_up, w_down, g_final)
```

```python
import functools
import math

import jax
import jax.numpy as jnp
from jax import lax
from jax.experimental import pallas as pl
from jax.experimental.pallas import tpu as pltpu

D_MODEL = 1024
PAST_LEN = 4096
RET_HEADS = 4
HEAD_DIM = 128
RET_WIDTH = RET_HEADS * HEAD_DIM
GM_GROUPS = 4
GM_WIDTH = GM_GROUPS * HEAD_DIM
GM_CHUNK = 128
D_FF = 4 * D_MODEL
ROPE_BASE = 10000.0
EPS = 1e-6
Q_SCALE = HEAD_DIM ** -0.5

_OFF_Q, _OFF_K, _OFF_V, _OFF_GATE, _OFF_U, _OFF_GV = (i * RET_WIDTH for i in range(6))

_VMEM_LIMIT_BYTES = 56 * 1024 * 1024
_FFN_COL_CHUNK = 1024
_FFN_TILE_ROWS = 1024
_FFN_GROUP_ROWS = 1024
_PROMPT_TILE_ROWS = 1024
_PROMPT_GROUP_ROWS = (512, 512)
_SAMPLE_GROUP_ROWS = (512, 512)
_SAMPLE_STREAMS_PER_TILE = 32
_SMALL_TASK_BATCH = 4

_BF16 = jnp.bfloat16
_F32 = jnp.float32


def _rms_scale(x, g):
    ms = jnp.mean(x * x, axis=-1, keepdims=True)
    return x * lax.rsqrt(ms + EPS) * g


def _gelu_tanh(x):
    c = math.sqrt(2.0 / math.pi)
    return 0.5 * x * (1.0 + jnp.tanh(c * (x + 0.044715 * (x * x * x))))


def _dot(a, b):
    return jnp.dot(a, b, preferred_element_type=_F32)


def _interleave(first, second):
    if not second:
        return list(first)
    if not first:
        return list(second)
    out, j = [], 0
    for i, task in enumerate(first):
        out.append(task)
        want = ((i + 1) * len(second)) // len(first)
        while j < want:
            out.append(second[j])
            j += 1
    return out


def _mixer_kernel(*refs, nb, T, C, group_rows, has_state, want_gv):
    it = iter(refs)
    x_ref = next(it)
    s0_ref = next(it) if has_state else None
    (gmix_ref, win_ref, cos_ref, sin_ref, decay_ref, cross_ref, kdec_ref, retgn_ref,
     lng_ref, lnb_ref, gmw_ref, gmb_ref, wout_ref) = (next(it) for _ in range(13))
    for _ in range(2 if want_gv else 1):
        next(it)
    xo_ref = next(it)
    so_ref = next(it)
    gv_ref = next(it) if want_gv else None
    q_s, k_s, v_s, gate_s, u_s, gvb_s, mix_s = (next(it) for _ in range(7))

    R = nb * T
    n_sub = len(group_rows)
    bounds = [sum(group_rows[:j]) for j in range(n_sub + 1)]
    assert bounds[-1] == R and all(rows % C == 0 for rows in group_rows)

    @pl.when(pl.program_id(1) == 0)
    def _():
        if has_state:
            so_ref[...] = s0_ref[...]
        else:
            so_ref[...] = jnp.zeros_like(so_ref)

    row_id = lax.broadcasted_iota(jnp.int32, (C, C), 0)
    col_id = lax.broadcasted_iota(jnp.int32, (C, C), 1)
    causal = row_id >= col_id
    ws = [jnp.where(causal, gmw_ref[g], 0.0).astype(_BF16) for g in range(GM_GROUPS)]

    def projection_tasks(j):
        rows = slice(bounds[j], bounds[j + 1])
        val = {}

        def norm():
            val['h'] = _rms_scale(x_ref[rows, :], gmix_ref[...]).astype(_BF16)

        def proj(name, off):
            def task():
                val[name] = _dot(val['h'], win_ref[:, off:off + RET_WIDTH])
            return task

        def rope_to(dst, name):
            def task():
                cos = cos_ref[rows, :]
                sin = sin_ref[rows, :]
                p = val.pop(name)
                for hd in range(RET_HEADS):
                    sl = slice(hd * HEAD_DIM, (hd + 1) * HEAD_DIM)
                    ph = p[:, sl]
                    dst[rows, sl] = (ph * cos + pltpu.roll(ph, HEAD_DIM // 2, 1) * sin).astype(_BF16)
            return task

        def u_act():
            u_s[rows, :] = _gelu_tanh(val.pop('u'))

        def gv_act():
            gvp = _gelu_tanh(val.pop('gv'))
            mu = jnp.mean(gvp, axis=-1, keepdims=True)
            gc = gvp - mu
            gv = gc * lax.rsqrt(jnp.mean(gc * gc, axis=-1, keepdims=True) + EPS)
            gv = gv * lng_ref[...] + lnb_ref[...]
            if want_gv:
                gv_ref[rows, :] = gv
            gvb_s[rows, :] = gv.astype(_BF16)

        def gate_act():
            pgate = val.pop('gate')
            gate_s[rows, :] = pgate * jax.nn.sigmoid(pgate) * retgn_ref[...]

        def v_act():
            v_s[rows, :] = val.pop('v').astype(_BF16)

        return [[norm, proj('k', _OFF_K)],
                [proj('q', _OFF_Q)],
                [proj('v', _OFF_V), rope_to(k_s, 'k')],
                [proj('gate', _OFF_GATE), rope_to(q_s, 'q')],
                [proj('u', _OFF_U), v_act],
                [proj('gv', _OFF_GV), gate_act, u_act, gv_act]]

    def retention_scores(stream, r0, hd, val):
        rows = slice(r0, r0 + C)
        sl = slice(hd * HEAD_DIM, (hd + 1) * HEAD_DIM)
        qh = q_s[rows, sl]
        kh = k_s[rows, sl]
        state = so_ref[stream, hd]
        kt = kh.astype(_F32).T
        val['sc'] = _dot(qh, kt.astype(_BF16))
        val['cross'] = _dot(qh, state.astype(_BF16))
        kv = _dot((kt * kdec_ref[hd]).astype(_BF16), v_s[rows, sl])
        so_ref[stream, hd] = state * math.exp(_log_gamma(hd) * C) + kv

    def retention_output(r0, hd, val):
        rows = slice(r0, r0 + C)
        sl = slice(hd * HEAD_DIM, (hd + 1) * HEAD_DIM)
        sc = (val.pop('sc') * decay_ref[hd]).astype(_BF16)
        o = _dot(sc, v_s[rows, sl]) + val.pop('cross') * cross_ref[hd]
        on = o * lax.rsqrt(jnp.mean(o * o, axis=-1, keepdims=True) + EPS)
        mix_s[rows, sl] = (gate_s[rows, sl] * on).astype(_BF16)

    def gmlp_block(r0, g):
        rows = slice(r0, r0 + C)
        sl = slice(g * HEAD_DIM, (g + 1) * HEAD_DIM)
        osl = slice(RET_WIDTH + g * HEAD_DIM, RET_WIDTH + (g + 1) * HEAD_DIM)
        z = _dot(ws[g], gvb_s[rows, sl]) + gmb_ref[g]
        mix_s[rows, osl] = (u_s[rows, sl] * z).astype(_BF16)

    def chunk_items(j):
        items = []
        for r0 in range(bounds[j], bounds[j + 1], C):
            for hd in range(RET_HEADS):
                val = {}
                items.append((functools.partial(retention_scores, r0 // T, r0, hd, val),
                              [functools.partial(gmlp_block, r0, hd),
                               functools.partial(retention_output, r0, hd, val)]))
        return items

    def output_tasks(j):
        rows = slice(bounds[j], bounds[j + 1])
        val = {}

        def first():
            val['y'] = _dot(mix_s[rows, :RET_WIDTH], wout_ref[:RET_WIDTH, :])

        def second():
            y = val.pop('y') + _dot(mix_s[rows, RET_WIDTH:], wout_ref[RET_WIDTH:, :])
            xo_ref[rows, :] = x_ref[rows, :] + y

        return [[first], [second]]

    def phase(bigs, items):
        tasks = []
        n_slots = max(len(bigs), -(-len(items) // _SMALL_TASK_BATCH))
        per = -(-len(items) // n_slots) if items else 0
        pending = []
        for slot in range(n_slots):
            if slot < len(bigs):
                tasks += bigs[slot]
            batch = items[slot * per:(slot + 1) * per]
            tasks += [scores for scores, _ in batch]
            tasks += pending
            pending = [task for _, later in batch for task in later]
        return tasks + pending

    program = []
    for p in range(n_sub + 2):
        bigs = _interleave(projection_tasks(p) if p < n_sub else [],
                           output_tasks(p - 2) if 0 <= p - 2 < n_sub else [])
        program += phase(bigs, chunk_items(p - 1) if 0 <= p - 1 < n_sub else [])
    for task in program:
        task()


def _ffn_kernel(x_ref, g_ref, wup_ref, wdn_ref, gfin_ref, o_ref, hid_s, *, n_sub, final):
    M = x_ref.shape[0] // n_sub
    n_col = D_FF // _FFN_COL_CHUNK

    def group_tasks(j):
        rows = slice(j * M, (j + 1) * M)
        val = {}

        def norm():
            val['h'] = _rms_scale(x_ref[rows, :], g_ref[...]).astype(_BF16)

        def up(n):
            def task():
                cols = slice(n * _FFN_COL_CHUNK, (n + 1) * _FFN_COL_CHUNK)
                a = jnp.maximum(_dot(val['h'], wup_ref[:, cols]), 0.0)
                hid_s[rows, cols] = (a * a).astype(_BF16)
            return task

        def down():
            y = x_ref[rows, :] + _dot(hid_s[rows, :], wdn_ref[...])
            if final:
                y = _rms_scale(y, gfin_ref[...])
            o_ref[rows, :] = y

        return norm, [up(n) for n in range(n_col)] + [down]

    groups = [group_tasks(j) for j in range(n_sub)]
    groups[0][0]()
    for j, (_, matmuls) in enumerate(groups):
        for n, task in enumerate(matmuls):
            task()
            if n == 0 and j + 1 < n_sub:
                groups[j + 1][0]()


def _log_gamma(hd):
    return math.log(1.0 - 2.0 ** (-5.0 - hd))


def _retention_constants(C):
    lg = jnp.asarray([_log_gamma(hd) for hd in range(RET_HEADS)], _F32)
    idx = jnp.arange(C, dtype=_F32)
    diff = idx[:, None] - idx[None, :]
    decay = jnp.where(diff >= 0, jnp.exp(lg[:, None, None] * jnp.maximum(diff, 0.0)), 0.0)
    cross = jnp.exp(lg[:, None] * (idx[None, :] + 1.0))
    kdec = jnp.exp(lg[:, None] * (C - 1.0 - idx[None, :]))
    bcast = lambda a: jnp.broadcast_to(a[:, :, None], (RET_HEADS, C, HEAD_DIM))
    kdec_t = jnp.broadcast_to(kdec[:, None, :], (RET_HEADS, HEAD_DIM, C))
    return decay * Q_SCALE, bcast(cross * Q_SCALE), kdec_t


def _rope_tables(pos):
    half = HEAD_DIM // 2
    inv_freq = ROPE_BASE ** (-jnp.arange(half, dtype=_F32) / half)
    ang = pos[:, None] * inv_freq[None, :]
    cos, sin = jnp.cos(ang), jnp.sin(ang)
    return jnp.concatenate([cos, cos], axis=-1), jnp.concatenate([-sin, sin], axis=-1)


def _const_spec(shape):
    return pl.BlockSpec(shape, lambda *_: (0,) * len(shape), pipeline_mode=pl.Buffered(1))


def _layer_spec(shape, layer):
    return pl.BlockSpec((None,) + shape, lambda *_: (layer, 0, 0), pipeline_mode=pl.Buffered(1))


def _mixer(x2d, s0, acc, lw, tables, *, layer, n_streams, S, nb, T, C, group_rows):
    has_state = s0 is not None
    want_gv = len(acc) == 2
    assert S % T == 0 and T % C == 0 and n_streams % nb == 0
    assert C == min(S, GM_CHUNK)
    assert nb == 1 or S == T
    R = nb * T
    n_t = S // T
    cos, sin, decay, cross, kdec = tables
    gmw = lw['gm_w'][:, :C, :C]
    gmb = jnp.broadcast_to(lw['gm_b'][:, :C, None], (GM_GROUPS, C, HEAD_DIM))

    grid = (n_streams // nb, n_t)
    row_spec = lambda width: pl.BlockSpec((R, width), lambda b, t: (b * n_t + t, 0))
    state_spec = pl.BlockSpec((None, nb, RET_HEADS, HEAD_DIM, HEAD_DIM),
                              lambda b, t: (layer, b, 0, 0, 0))
    rope_spec = pl.BlockSpec((R, HEAD_DIM), lambda b, t: (t, 0))
    in_specs = [row_spec(D_MODEL)]
    args = [x2d]
    if has_state:
        in_specs.append(state_spec)
        args.append(s0)
    in_specs += [
        _const_spec((1, D_MODEL)),
        _layer_spec((D_MODEL, 6 * RET_WIDTH), layer),
        rope_spec,
        rope_spec,
        _const_spec((RET_HEADS, C, C)),
        _const_spec((RET_HEADS, C, HEAD_DIM)),
        _const_spec((RET_HEADS, HEAD_DIM, C)),
        _const_spec((1, RET_WIDTH)),
        _const_spec((1, GM_WIDTH)),
        _const_spec((1, GM_WIDTH)),
        _const_spec((GM_GROUPS, C, C)),
        _const_spec((GM_GROUPS, C, HEAD_DIM)),
        _layer_spec((D_MODEL, D_MODEL), layer),
    ]
    args += [lw['g_mix'], lw['w_in'], cos, sin, decay, cross, kdec, lw['ret_gn'],
             lw['gm_ln_g'], lw['gm_ln_b'], gmw, gmb, lw['w_out']]
    first_acc = len(args)
    in_specs += [pl.BlockSpec(memory_space=pl.ANY)] * len(acc)
    args += list(acc)

    out_shape = [jax.ShapeDtypeStruct(x2d.shape, _F32)]
    out_shape += [jax.ShapeDtypeStruct(a.shape, _F32) for a in acc]
    out_specs = [row_spec(D_MODEL), state_spec]
    if want_gv:
        out_specs.append(pl.BlockSpec((None, R, GM_WIDTH), lambda b, t: (layer, b * n_t + t, 0)))

    scratch = [pltpu.VMEM((R, RET_WIDTH), _BF16),
               pltpu.VMEM((R, RET_WIDTH), _BF16),
               pltpu.VMEM((R, RET_WIDTH), _BF16),
               pltpu.VMEM((R, RET_WIDTH), _F32),
               pltpu.VMEM((R, GM_WIDTH), _F32),
               pltpu.VMEM((R, GM_WIDTH), _BF16),
               pltpu.VMEM((R, D_MODEL), _BF16)]

    kern = functools.partial(_mixer_kernel, nb=nb, T=T, C=C, group_rows=group_rows,
                             has_state=has_state, want_gv=want_gv)
    return pl.pallas_call(
        kern,
        grid=grid,
        in_specs=in_specs,
        out_specs=out_specs,
        out_shape=out_shape,
        scratch_shapes=scratch,
        input_output_aliases={first_acc + n: 1 + n for n in range(len(acc))},
        compiler_params=pltpu.CompilerParams(
            dimension_semantics=("arbitrary",) * len(grid),
            vmem_limit_bytes=_VMEM_LIMIT_BYTES),
        name="mixer_state" if has_state else "mixer",
    )(*args)


def _ffn(x2d, lw, g_final, *, layer, R, final):
    n_rows = x2d.shape[0]
    assert n_rows % R == 0
    n_sub = max(1, R // _FFN_GROUP_ROWS)
    const = lambda shape: pl.BlockSpec(shape, lambda r: (0,) * len(shape),
                                       pipeline_mode=pl.Buffered(1))
    row_spec = pl.BlockSpec((R, D_MODEL), lambda r: (r, 0))
    return pl.pallas_call(
        functools.partial(_ffn_kernel, n_sub=n_sub, final=final),
        grid=(n_rows // R,),
        in_specs=[row_spec, const((1, D_MODEL)), _layer_spec((D_MODEL, D_FF), layer),
                  _layer_spec((D_FF, D_MODEL), layer), const((1, D_MODEL))],
        out_specs=row_spec,
        out_shape=jax.ShapeDtypeStruct(x2d.shape, _F32),
        scratch_shapes=[pltpu.VMEM((R, D_FF), _BF16)],
        compiler_params=pltpu.CompilerParams(
            dimension_semantics=("arbitrary",),
            vmem_limit_bytes=_VMEM_LIMIT_BYTES),
        name="ffn_final" if final else "ffn",
    )(x2d, lw['g_ffn'], lw['w_up'], lw['w_down'], g_final)


def _pick_tile(S, pref):
    t = min(S, pref)
    while S % t:
        t //= 2
    return t


def _forward(x_prompt, x_sample, state_ret, g_mix, w_in, ret_gn, gm_ln_g, gm_ln_b, gm_w, gm_b,
             w_out, g_ffn, w_up, w_down, g_final):
    B, S, D = x_prompt.shape
    DB, DS, _ = x_sample.shape
    depth = w_in.shape[0]

    T_p = _pick_tile(S, _PROMPT_TILE_ROWS)
    C_p = min(S, GM_CHUNK)
    groups_p = _PROMPT_GROUP_ROWS if sum(_PROMPT_GROUP_ROWS) == T_p else (T_p,)
    nb_s = math.gcd(DB, _SAMPLE_STREAMS_PER_TILE)
    groups_s = _SAMPLE_GROUP_ROWS if sum(_SAMPLE_GROUP_ROWS) == nb_s * DS else (nb_s * DS,)

    cos_p, sin_p = _rope_tables(jnp.arange(S, dtype=_F32))
    cos_s, sin_s = _rope_tables(PAST_LEN + jnp.arange(DS, dtype=_F32))
    cos_s, sin_s = jnp.tile(cos_s, (nb_s, 1)), jnp.tile(sin_s, (nb_s, 1))
    tab_p = (cos_p, sin_p) + _retention_constants(C_p)
    tab_s = (cos_s, sin_s) + _retention_constants(DS)

    row = lambda a: a.reshape(1, -1)
    g_fin = row(g_final)
    hp = x_prompt.reshape(B * S, D)
    hs = x_sample.reshape(DB * DS, D)
    R_ffn_p = _pick_tile(B * S, _FFN_TILE_ROWS)
    R_ffn_s = _pick_tile(DB * DS, _FFN_TILE_ROWS)

    w_in_b, w_out_b = w_in.astype(_BF16), w_out.astype(_BF16)
    w_up_b, w_down_b = w_up.astype(_BF16), w_down.astype(_BF16)

    state_p = jnp.zeros((depth, B, RET_HEADS, HEAD_DIM, HEAD_DIM), _F32)
    state_s = jnp.zeros((depth, DB, RET_HEADS, HEAD_DIM, HEAD_DIM), _F32)
    gv_s = jnp.zeros((depth, DB * DS, GM_WIDTH), _F32)
    for l in range(depth):
        lw = dict(g_mix=row(g_mix[l]), w_in=w_in_b, ret_gn=row(ret_gn[l]),
                  gm_ln_g=row(gm_ln_g[l]), gm_ln_b=row(gm_ln_b[l]), gm_w=gm_w[l], gm_b=gm_b[l],
                  w_out=w_out_b, g_ffn=row(g_ffn[l]), w_up=w_up_b, w_down=w_down_b)
        final = l == depth - 1
        hp, state_p = _mixer(hp, None, (state_p,), lw, tab_p, layer=l, n_streams=B, S=S, nb=1,
                             T=T_p, C=C_p, group_rows=groups_p)
        hp = _ffn(hp, lw, g_fin, layer=l, R=R_ffn_p, final=final)
        hs, state_s, gv_s = _mixer(hs, state_ret, (state_s, gv_s), lw, tab_s, layer=l,
                                   n_streams=DB, S=DS, nb=nb_s, T=DS, C=DS,
                                   group_rows=groups_s)
        hs = _ffn(hs, lw, g_fin, layer=l, R=R_ffn_s, final=final)

    return (hp.reshape(B, S, D), hs.reshape(DB, DS, D), state_p, state_s,
            gv_s.reshape(depth, DB, DS, GM_WIDTH))


def kernel(x_prompt, x_sample, state_ret, g_mix, w_in, ret_gn, gm_ln_g, gm_ln_b, gm_w, gm_b,
           w_out, g_ffn, w_up, w_down, g_final):
    return _forward(x_prompt, x_sample, state_ret, g_mix, w_in, ret_gn, gm_ln_g, gm_ln_b, gm_w,
                    gm_b, w_out, g_ffn, w_up, w_down, g_final)
```

```python
import functools
import math

import jax
import jax.numpy as jnp
from jax import lax
from jax.experimental import pallas as pl
from jax.experimental.pallas import tpu as pltpu

D_MODEL = 1024
PAST_LEN = 4096
RET_HEADS = 4
HEAD_DIM = 128
RET_WIDTH = RET_HEADS * HEAD_DIM
GM_GROUPS = 4
GM_WIDTH = GM_GROUPS * HEAD_DIM
GM_CHUNK = 128
D_FF = 4 * D_MODEL
ROPE_BASE = 10000.0
EPS = 1e-6
Q_SCALE = HEAD_DIM ** -0.5

_OFF_Q, _OFF_K, _OFF_V, _OFF_GATE, _OFF_U, _OFF_GV = (i * RET_WIDTH for i in range(6))

_VMEM_LIMIT_BYTES = 56 * 1024 * 1024
_FFN_COL_CHUNK = 1024
_FFN_TILE_ROWS = 1024
_FFN_GROUP_ROWS = 512
_PROMPT_TILE_ROWS = 1024
_PROMPT_GROUP_ROWS = (512, 512)
_SAMPLE_GROUP_ROWS = (512, 512)
_SAMPLE_STREAMS_PER_TILE = 32
_SMALL_TASK_BATCH = 4

_BF16 = jnp.bfloat16
_F32 = jnp.float32


def _rms_scale(x, g):
    ms = jnp.mean(x * x, axis=-1, keepdims=True)
    return x * lax.rsqrt(ms + EPS) * g


def _gelu_tanh(x):
    c = math.sqrt(2.0 / math.pi)
    return 0.5 * x * (1.0 + jnp.tanh(c * (x + 0.044715 * (x * x * x))))


def _dot(a, b):
    return jnp.dot(a, b, preferred_element_type=_F32)


def _interleave(first, second):
    if not second:
        return list(first)
    if not first:
        return list(second)
    out, j = [], 0
    for i, task in enumerate(first):
        out.append(task)
        want = ((i + 1) * len(second)) // len(first)
        while j < want:
            out.append(second[j])
            j += 1
    return out


def _mixer_kernel(*refs, nb, T, C, group_rows, has_state, want_gv):
    it = iter(refs)
    x_ref = next(it)
    s0_ref = next(it) if has_state else None
    (gmix_ref, win_ref, cos_ref, sin_ref, decay_ref, cross_ref, kdec_ref, retgn_ref,
     lng_ref, lnb_ref, gmw_ref, gmb_ref, wout_ref) = (next(it) for _ in range(13))
    for _ in range(2 if want_gv else 1):
        next(it)
    xo_ref = next(it)
    so_ref = next(it)
    gv_ref = next(it) if want_gv else None
    q_s, k_s, v_s, gate_s, u_s, gvb_s, mix_s = (next(it) for _ in range(7))

    R = nb * T
    n_sub = len(group_rows)
    bounds = [sum(group_rows[:j]) for j in range(n_sub + 1)]
    assert bounds[-1] == R and all(rows % C == 0 for rows in group_rows)

    @pl.when(pl.program_id(1) == 0)
    def _():
        if has_state:
            so_ref[...] = s0_ref[...]
        else:
            so_ref[...] = jnp.zeros_like(so_ref)

    row_id = lax.broadcasted_iota(jnp.int32, (C, C), 0)
    col_id = lax.broadcasted_iota(jnp.int32, (C, C), 1)
    causal = row_id >= col_id
    ws = [jnp.where(causal, gmw_ref[g], 0.0).astype(_BF16) for g in range(GM_GROUPS)]

    def projection_tasks(j):
        rows = slice(bounds[j], bounds[j + 1])
        val = {}

        def norm():
            val['h'] = _rms_scale(x_ref[rows, :], gmix_ref[...]).astype(_BF16)

        def proj(name, off):
            def task():
                val[name] = _dot(val['h'], win_ref[:, off:off + RET_WIDTH])
            return task

        def rope_to(dst, name):
            def task():
                cos = cos_ref[rows, :]
                sin = sin_ref[rows, :]
                p = val.pop(name)
                for hd in range(RET_HEADS):
                    sl = slice(hd * HEAD_DIM, (hd + 1) * HEAD_DIM)
                    ph = p[:, sl]
                    dst[rows, sl] = (ph * cos + pltpu.roll(ph, HEAD_DIM // 2, 1) * sin).astype(_BF16)
            return task

        def u_act():
            u_s[rows, :] = _gelu_tanh(val.pop('u'))

        def gv_act():
            gvp = _gelu_tanh(val.pop('gv'))
            mu = jnp.mean(gvp, axis=-1, keepdims=True)
            gc = gvp - mu
            gv = gc * lax.rsqrt(jnp.mean(gc * gc, axis=-1, keepdims=True) + EPS)
            gv = gv * lng_ref[...] + lnb_ref[...]
            if want_gv:
                gv_ref[rows, :] = gv
            gvb_s[rows, :] = gv.astype(_BF16)

        def gate_act():
            pgate = val.pop('gate')
            gate_s[rows, :] = pgate * jax.nn.sigmoid(pgate) * retgn_ref[...]

        def v_act():
            v_s[rows, :] = val.pop('v').astype(_BF16)

        return [[norm, proj('k', _OFF_K)],
                [proj('q', _OFF_Q)],
                [proj('v', _OFF_V), rope_to(k_s, 'k')],
                [proj('gate', _OFF_GATE), rope_to(q_s, 'q')],
                [proj('u', _OFF_U), v_act],
                [proj('gv', _OFF_GV), gate_act, u_act, gv_act]]

    def retention_scores(stream, r0, hd, val):
        rows = slice(r0, r0 + C)
        sl = slice(hd * HEAD_DIM, (hd + 1) * HEAD_DIM)
        qh = q_s[rows, sl]
        kh = k_s[rows, sl]
        state = so_ref[stream, hd]
        kt = kh.T
        val['sc'] = _dot(qh, kt)
        val['cross'] = _dot(qh, state.astype(_BF16))
        kv = _dot((kt.astype(_F32) * kdec_ref[hd]).astype(_BF16), v_s[rows, sl])
        so_ref[stream, hd] = state * math.exp(_log_gamma(hd) * C) + kv

    def retention_output(r0, hd, val):
        rows = slice(r0, r0 + C)
        sl = slice(hd * HEAD_DIM, (hd + 1) * HEAD_DIM)
        sc = (val.pop('sc') * decay_ref[hd]).astype(_BF16)
        o = _dot(sc, v_s[rows, sl]) + val.pop('cross') * cross_ref[hd]
        on = o * lax.rsqrt(jnp.mean(o * o, axis=-1, keepdims=True) + EPS)
        mix_s[rows, sl] = (gate_s[rows, sl] * on).astype(_BF16)

    def gmlp_block(r0, g):
        rows = slice(r0, r0 + C)
        sl = slice(g * HEAD_DIM, (g + 1) * HEAD_DIM)
        osl = slice(RET_WIDTH + g * HEAD_DIM, RET_WIDTH + (g + 1) * HEAD_DIM)
        z = _dot(ws[g], gvb_s[rows, sl]) + gmb_ref[g]
        mix_s[rows, osl] = (u_s[rows, sl] * z).astype(_BF16)

    def chunk_items(j):
        items = []
        for r0 in range(bounds[j], bounds[j + 1], C):
            for hd in range(RET_HEADS):
                val = {}
                items.append((functools.partial(retention_scores, r0 // T, r0, hd, val),
                              [functools.partial(gmlp_block, r0, hd),
                               functools.partial(retention_output, r0, hd, val)]))
        return items

    def output_tasks(j):
        rows = slice(bounds[j], bounds[j + 1])
        val = {}

        def first():
            val['y'] = _dot(mix_s[rows, :RET_WIDTH], wout_ref[:RET_WIDTH, :])

        def second():
            y = val.pop('y') + _dot(mix_s[rows, RET_WIDTH:], wout_ref[RET_WIDTH:, :])
            xo_ref[rows, :] = x_ref[rows, :] + y

        return [[first], [second]]

    def phase(bigs, items):
        tasks = []
        n_slots = max(len(bigs), -(-len(items) // _SMALL_TASK_BATCH))
        per = -(-len(items) // n_slots) if items else 0
        pending = []
        for slot in range(n_slots):
            if slot < len(bigs):
                tasks += bigs[slot]
            batch = items[slot * per:(slot + 1) * per]
            tasks += [scores for scores, _ in batch]
            tasks += pending
            pending = [task for _, later in batch for task in later]
        return tasks + pending

    program = []
    for p in range(n_sub + 2):
        bigs = _interleave(projection_tasks(p) if p < n_sub else [],
                           output_tasks(p - 2) if 0 <= p - 2 < n_sub else [])
        program += phase(bigs, chunk_items(p - 1) if 0 <= p - 1 < n_sub else [])
    for task in program:
        task()


def _ffn_kernel(x_ref, g_ref, wup_ref, wdn_ref, gfin_ref, o_ref, hid_s, *, n_sub, final):
    M = x_ref.shape[0] // n_sub
    n_col = D_FF // _FFN_COL_CHUNK

    def group_tasks(j):
        rows = slice(j * M, (j + 1) * M)
        val = {}

        def norm():
            val['h'] = _rms_scale(x_ref[rows, :], g_ref[...]).astype(_BF16)

        def up(n):
            def task():
                cols = slice(n * _FFN_COL_CHUNK, (n + 1) * _FFN_COL_CHUNK)
                a = jnp.maximum(_dot(val['h'], wup_ref[:, cols]), 0.0)
                hid_s[rows, cols] = (a * a).astype(_BF16)
            return task

        def down():
            y = x_ref[rows, :] + _dot(hid_s[rows, :], wdn_ref[...])
            if final:
                y = _rms_scale(y, gfin_ref[...])
            o_ref[rows, :] = y

        return norm, [up(n) for n in range(n_col)] + [down]

    groups = [group_tasks(j) for j in range(n_sub)]
    groups[0][0]()
    for j, (_, matmuls) in enumerate(groups):
        for n, task in enumerate(matmuls):
            task()
            if n == 0 and j + 1 < n_sub:
                groups[j + 1][0]()


def _log_gamma(hd):
    return math.log(1.0 - 2.0 ** (-5.0 - hd))


def _retention_constants(C):
    lg = jnp.asarray([_log_gamma(hd) for hd in range(RET_HEADS)], _F32)
    idx = jnp.arange(C, dtype=_F32)
    diff = idx[:, None] - idx[None, :]
    decay = jnp.where(diff >= 0, jnp.exp(lg[:, None, None] * jnp.maximum(diff, 0.0)), 0.0)
    cross = jnp.exp(lg[:, None] * (idx[None, :] + 1.0))
    kdec = jnp.exp(lg[:, None] * (C - 1.0 - idx[None, :]))
    bcast = lambda a: jnp.broadcast_to(a[:, :, None], (RET_HEADS, C, HEAD_DIM))
    kdec_t = jnp.broadcast_to(kdec[:, None, :], (RET_HEADS, HEAD_DIM, C))
    return decay * Q_SCALE, bcast(cross * Q_SCALE), kdec_t


def _rope_tables(pos):
    half = HEAD_DIM // 2
    inv_freq = ROPE_BASE ** (-jnp.arange(half, dtype=_F32) / half)
    ang = pos[:, None] * inv_freq[None, :]
    cos, sin = jnp.cos(ang), jnp.sin(ang)
    return jnp.concatenate([cos, cos], axis=-1), jnp.concatenate([-sin, sin], axis=-1)


def _const_spec(shape):
    return pl.BlockSpec(shape, lambda *_: (0,) * len(shape), pipeline_mode=pl.Buffered(1))


def _layer_spec(shape, layer):
    return pl.BlockSpec((None,) + shape, lambda *_: (layer, 0, 0), pipeline_mode=pl.Buffered(1))


def _mixer(x2d, s0, acc, lw, tables, *, layer, n_streams, S, nb, T, C, group_rows):
    has_state = s0 is not None
    want_gv = len(acc) == 2
    assert S % T == 0 and T % C == 0 and n_streams % nb == 0
    assert C == min(S, GM_CHUNK)
    assert nb == 1 or S == T
    R = nb * T
    n_t = S // T
    cos, sin, decay, cross, kdec = tables
    gmw = lw['gm_w'][:, :C, :C]
    gmb = jnp.broadcast_to(lw['gm_b'][:, :C, None], (GM_GROUPS, C, HEAD_DIM))

    grid = (n_streams // nb, n_t)
    row_spec = lambda width: pl.BlockSpec((R, width), lambda b, t: (b * n_t + t, 0))
    state_spec = pl.BlockSpec((None, nb, RET_HEADS, HEAD_DIM, HEAD_DIM),
                              lambda b, t: (layer, b, 0, 0, 0))
    rope_spec = pl.BlockSpec((R, HEAD_DIM), lambda b, t: (t, 0))
    in_specs = [row_spec(D_MODEL)]
    args = [x2d]
    if has_state:
        in_specs.append(state_spec)
        args.append(s0)
    in_specs += [
        _const_spec((1, D_MODEL)),
        _layer_spec((D_MODEL, 6 * RET_WIDTH), layer),
        rope_spec,
        rope_spec,
        _const_spec((RET_HEADS, C, C)),
        _const_spec((RET_HEADS, C, HEAD_DIM)),
        _const_spec((RET_HEADS, HEAD_DIM, C)),
        _const_spec((1, RET_WIDTH)),
        _const_spec((1, GM_WIDTH)),
        _const_spec((1, GM_WIDTH)),
        _const_spec((GM_GROUPS, C, C)),
        _const_spec((GM_GROUPS, C, HEAD_DIM)),
        _layer_spec((D_MODEL, D_MODEL), layer),
    ]
    args += [lw['g_mix'], lw['w_in'], cos, sin, decay, cross, kdec, lw['ret_gn'],
             lw['gm_ln_g'], lw['gm_ln_b'], gmw, gmb, lw['w_out']]
    first_acc = len(args)
    in_specs += [pl.BlockSpec(memory_space=pl.ANY)] * len(acc)
    args += list(acc)

    out_shape = [jax.ShapeDtypeStruct(x2d.shape, _F32)]
    out_shape += [jax.ShapeDtypeStruct(a.shape, _F32) for a in acc]
    out_specs = [row_spec(D_MODEL), state_spec]
    if want_gv:
        out_specs.append(pl.BlockSpec((None, R, GM_WIDTH), lambda b, t: (layer, b * n_t + t, 0)))

    scratch = [pltpu.VMEM((R, RET_WIDTH), _BF16),
               pltpu.VMEM((R, RET_WIDTH), _BF16),
               pltpu.VMEM((R, RET_WIDTH), _BF16),
               pltpu.VMEM((R, RET_WIDTH), _F32),
               pltpu.VMEM((R, GM_WIDTH), _F32),
               pltpu.VMEM((R, GM_WIDTH), _BF16),
               pltpu.VMEM((R, D_MODEL), _BF16)]

    kern = functools.partial(_mixer_kernel, nb=nb, T=T, C=C, group_rows=group_rows,
                             has_state=has_state, want_gv=want_gv)
    return pl.pallas_call(
        kern,
        grid=grid,
        in_specs=in_specs,
        out_specs=out_specs,
        out_shape=out_shape,
        scratch_shapes=scratch,
        input_output_aliases={first_acc + n: 1 + n for n in range(len(acc))},
        compiler_params=pltpu.CompilerParams(
            dimension_semantics=("arbitrary",) * len(grid),
            vmem_limit_bytes=_VMEM_LIMIT_BYTES),
        name="mixer_state" if has_state else "mixer",
    )(*args)


def _ffn(x2d, lw, g_final, *, layer, R, final):
    n_rows = x2d.shape[0]
    assert n_rows % R == 0
    n_sub = max(1, R // _FFN_GROUP_ROWS)
    const = lambda shape: pl.BlockSpec(shape, lambda r: (0,) * len(shape),
                                       pipeline_mode=pl.Buffered(1))
    row_spec = pl.BlockSpec((R, D_MODEL), lambda r: (r, 0))
    return pl.pallas_call(
        functools.partial(_ffn_kernel, n_sub=n_sub, final=final),
        grid=(n_rows // R,),
        in_specs=[row_spec, const((1, D_MODEL)), _layer_spec((D_MODEL, D_FF), layer),
                  _layer_spec((D_FF, D_MODEL), layer), const((1, D_MODEL))],
        out_specs=row_spec,
        out_shape=jax.ShapeDtypeStruct(x2d.shape, _F32),
        scratch_shapes=[pltpu.VMEM((R, D_FF), _BF16)],
        compiler_params=pltpu.CompilerParams(
            dimension_semantics=("arbitrary",),
            vmem_limit_bytes=_VMEM_LIMIT_BYTES),
        name="ffn_final" if final else "ffn",
    )(x2d, lw['g_ffn'], lw['w_up'], lw['w_down'], g_final)


def _pick_tile(S, pref):
    t = min(S, pref)
    while S % t:
        t //= 2
    return t


def _forward(x_prompt, x_sample, state_ret, g_mix, w_in, ret_gn, gm_ln_g, gm_ln_b, gm_w, gm_b,
             w_out, g_ffn, w_up, w_down, g_final):
    B, S, D = x_prompt.shape
    DB, DS, _ = x_sample.shape
    depth = w_in.shape[0]

    T_p = _pick_tile(S, _PROMPT_TILE_ROWS)
    C_p = min(S, GM_CHUNK)
    groups_p = _PROMPT_GROUP_ROWS if sum(_PROMPT_GROUP_ROWS) == T_p else (T_p,)
    nb_s = math.gcd(DB, _SAMPLE_STREAMS_PER_TILE)
    groups_s = _SAMPLE_GROUP_ROWS if sum(_SAMPLE_GROUP_ROWS) == nb_s * DS else (nb_s * DS,)

    cos_p, sin_p = _rope_tables(jnp.arange(S, dtype=_F32))
    cos_s, sin_s = _rope_tables(PAST_LEN + jnp.arange(DS, dtype=_F32))
    cos_s, sin_s = jnp.tile(cos_s, (nb_s, 1)), jnp.tile(sin_s, (nb_s, 1))
    tab_p = (cos_p, sin_p) + _retention_constants(C_p)
    tab_s = (cos_s, sin_s) + _retention_constants(DS)

    row = lambda a: a.reshape(1, -1)
    g_fin = row(g_final)
    hp = x_prompt.reshape(B * S, D)
    hs = x_sample.reshape(DB * DS, D)
    R_ffn_p = _pick_tile(B * S, _FFN_TILE_ROWS)
    R_ffn_s = _pick_tile(DB * DS, _FFN_TILE_ROWS)

    w_in_b, w_out_b = w_in.astype(_BF16), w_out.astype(_BF16)
    w_up_b, w_down_b = w_up.astype(_BF16), w_down.astype(_BF16)

    state_p = jnp.zeros((depth, B, RET_HEADS, HEAD_DIM, HEAD_DIM), _F32)
    state_s = jnp.zeros((depth, DB, RET_HEADS, HEAD_DIM, HEAD_DIM), _F32)
    gv_s = jnp.zeros((depth, DB * DS, GM_WIDTH), _F32)
    for l in range(depth):
        lw = dict(g_mix=row(g_mix[l]), w_in=w_in_b, ret_gn=row(ret_gn[l]),
                  gm_ln_g=row(gm_ln_g[l]), gm_ln_b=row(gm_ln_b[l]), gm_w=gm_w[l], gm_b=gm_b[l],
                  w_out=w_out_b, g_ffn=row(g_ffn[l]), w_up=w_up_b, w_down=w_down_b)
        final = l == depth - 1
        hp, state_p = _mixer(hp, None, (state_p,), lw, tab_p, layer=l, n_streams=B, S=S, nb=1,
                             T=T_p, C=C_p, group_rows=groups_p)
        hp = _ffn(hp, lw, g_fin, layer=l, R=R_ffn_p, final=final)
        hs, state_s, gv_s = _mixer(hs, state_ret, (state_s, gv_s), lw, tab_s, layer=l,
                                   n_streams=DB, S=DS, nb=nb_s, T=DS, C=DS,
                                   group_rows=groups_s)
        hs = _ffn(hs, lw, g_fin, layer=l, R=R_ffn_s, final=final)

    return (hp.reshape(B, S, D), hs.reshape(DB, DS, D), state_p, state_s,
            gv_s.reshape(depth, DB, DS, GM_WIDTH))


def kernel(x_prompt, x_sample, state_ret, g_mix, w_in, ret_gn, gm_ln_g, gm_ln_b, gm_w, gm_b,
           w_out, g_ffn, w_up, w_down, g_final):
    return _forward(x_prompt, x_sample, state_ret, g_mix, w_in, ret_gn, gm_ln_g, gm_ln_b, gm_w,
                    gm_b, w_out, g_ffn, w_up, w_down, g_final)
```

```python
import functools
import math

import jax
import jax.numpy as jnp
from jax import lax
from jax.experimental import pallas as pl
from jax.experimental.pallas import tpu as pltpu

D_MODEL = 1024
PAST_LEN = 4096
RET_HEADS = 4
HEAD_DIM = 128
RET_WIDTH = RET_HEADS * HEAD_DIM
GM_GROUPS = 4
GM_WIDTH = GM_GROUPS * HEAD_DIM
GM_CHUNK = 128
D_FF = 4 * D_MODEL
ROPE_BASE = 10000.0
EPS = 1e-6
Q_SCALE = HEAD_DIM ** -0.5

_OFF_Q, _OFF_K, _OFF_V, _OFF_GATE, _OFF_U, _OFF_GV = (i * RET_WIDTH for i in range(6))

_VMEM_LIMIT_BYTES = 56 * 1024 * 1024
_FFN_COL_CHUNK = 1024
_FFN_TILE_ROWS = 1024
_FFN_GROUP_ROWS = 512
_PROMPT_TILE_ROWS = 1024
_PROMPT_GROUP_ROWS = (512, 512)
_SAMPLE_GROUP_ROWS = (512, 512)
_SAMPLE_STREAMS_PER_TILE = 32
_ACTIVATION_BUFFERS = 2
_SMALL_TASK_BATCH = 4

_BF16 = jnp.bfloat16
_F32 = jnp.float32


def _rms_scale(x, g):
    ms = jnp.mean(x * x, axis=-1, keepdims=True)
    return x * lax.rsqrt(ms + EPS) * g


def _gelu_tanh(x):
    c = math.sqrt(2.0 / math.pi)
    return 0.5 * x * (1.0 + jnp.tanh(c * (x + 0.044715 * (x * x * x))))


def _dot(a, b):
    return jnp.dot(a, b, preferred_element_type=_F32)


def _interleave(first, second):
    if not second:
        return list(first)
    if not first:
        return list(second)
    out, j = [], 0
    for i, task in enumerate(first):
        out.append(task)
        want = ((i + 1) * len(second)) // len(first)
        while j < want:
            out.append(second[j])
            j += 1
    return out


def _mixer_kernel(*refs, nb, T, C, group_rows, has_state, want_gv):
    it = iter(refs)
    x_ref = next(it)
    s0_ref = next(it) if has_state else None
    (gmix_ref, win_ref, cos_ref, sin_ref, decay_ref, cross_ref, kdec_ref, retgn_ref,
     lng_ref, lnb_ref, gmw_ref, gmb_ref, wout_ref) = (next(it) for _ in range(13))
    for _ in range(2 if want_gv else 1):
        next(it)
    xo_ref = next(it)
    so_ref = next(it)
    gv_ref = next(it) if want_gv else None
    q_s, k_s, v_s, gate_s, u_s, gvb_s, mix_s = (next(it) for _ in range(7))

    R = nb * T
    n_sub = len(group_rows)
    bounds = [sum(group_rows[:j]) for j in range(n_sub + 1)]
    assert bounds[-1] == R and all(rows % C == 0 for rows in group_rows)

    @pl.when(pl.program_id(1) == 0)
    def _():
        if has_state:
            so_ref[...] = s0_ref[...]
        else:
            so_ref[...] = jnp.zeros_like(so_ref)

    row_id = lax.broadcasted_iota(jnp.int32, (C, C), 0)
    col_id = lax.broadcasted_iota(jnp.int32, (C, C), 1)
    causal = row_id >= col_id
    ws = [jnp.where(causal, gmw_ref[g], 0.0).astype(_BF16) for g in range(GM_GROUPS)]

    def projection_tasks(j):
        rows = slice(bounds[j], bounds[j + 1])
        val = {}

        def norm():
            val['h'] = _rms_scale(x_ref[rows, :], gmix_ref[...]).astype(_BF16)

        def proj(name, off):
            def task():
                val[name] = _dot(val['h'], win_ref[:, off:off + RET_WIDTH])
            return task

        def rope_to(dst, name):
            def task():
                cos = cos_ref[rows, :]
                sin = sin_ref[rows, :]
                p = val.pop(name)
                for hd in range(RET_HEADS):
                    sl = slice(hd * HEAD_DIM, (hd + 1) * HEAD_DIM)
                    ph = p[:, sl]
                    dst[rows, sl] = (ph * cos + pltpu.roll(ph, HEAD_DIM // 2, 1) * sin).astype(_BF16)
            return task

        def u_act():
            u_s[rows, :] = _gelu_tanh(val.pop('u'))

        def gv_act():
            gvp = _gelu_tanh(val.pop('gv'))
            mu = jnp.mean(gvp, axis=-1, keepdims=True)
            gc = gvp - mu
            gv = gc * lax.rsqrt(jnp.mean(gc * gc, axis=-1, keepdims=True) + EPS)
            gv = gv * lng_ref[...] + lnb_ref[...]
            if want_gv:
                gv_ref[rows, :] = gv
            gvb_s[rows, :] = gv.astype(_BF16)

        def gate_act():
            pgate = val.pop('gate')
            gate_s[rows, :] = pgate * jax.nn.sigmoid(pgate) * retgn_ref[...]

        def v_act():
            v_s[rows, :] = val.pop('v').astype(_BF16)

        return [[norm, proj('k', _OFF_K)],
                [proj('q', _OFF_Q)],
                [proj('v', _OFF_V), rope_to(k_s, 'k')],
                [proj('gate', _OFF_GATE), rope_to(q_s, 'q')],
                [proj('u', _OFF_U), v_act],
                [proj('gv', _OFF_GV), gate_act, u_act, gv_act]]

    def retention_scores(stream, r0, hd, val):
        rows = slice(r0, r0 + C)
        sl = slice(hd * HEAD_DIM, (hd + 1) * HEAD_DIM)
        qh = q_s[rows, sl]
        kh = k_s[rows, sl]
        state = so_ref[stream, hd]
        kt = kh.T
        val['sc'] = _dot(qh, kt)
        val['cross'] = _dot(qh, state.astype(_BF16))
        kv = _dot((kt.astype(_F32) * kdec_ref[hd]).astype(_BF16), v_s[rows, sl])
        so_ref[stream, hd] = state * math.exp(_log_gamma(hd) * C) + kv

    def retention_output(r0, hd, val):
        rows = slice(r0, r0 + C)
        sl = slice(hd * HEAD_DIM, (hd + 1) * HEAD_DIM)
        sc = (val.pop('sc') * decay_ref[hd]).astype(_BF16)
        o = _dot(sc, v_s[rows, sl]) + val.pop('cross') * cross_ref[hd]
        on = o * lax.rsqrt(jnp.mean(o * o, axis=-1, keepdims=True) + EPS)
        mix_s[rows, sl] = (gate_s[rows, sl] * on).astype(_BF16)

    def gmlp_block(r0, g):
        rows = slice(r0, r0 + C)
        sl = slice(g * HEAD_DIM, (g + 1) * HEAD_DIM)
        osl = slice(RET_WIDTH + g * HEAD_DIM, RET_WIDTH + (g + 1) * HEAD_DIM)
        z = _dot(ws[g], gvb_s[rows, sl]) + gmb_ref[g]
        mix_s[rows, osl] = (u_s[rows, sl] * z).astype(_BF16)

    def chunk_items(j):
        items = []
        for r0 in range(bounds[j], bounds[j + 1], C):
            for hd in range(RET_HEADS):
                val = {}
                items.append((functools.partial(retention_scores, r0 // T, r0, hd, val),
                              [functools.partial(gmlp_block, r0, hd),
                               functools.partial(retention_output, r0, hd, val)]))
        return items

    def output_tasks(j):
        rows = slice(bounds[j], bounds[j + 1])
        val = {}

        def first():
            val['y'] = _dot(mix_s[rows, :RET_WIDTH], wout_ref[:RET_WIDTH, :])

        def second():
            y = val.pop('y') + _dot(mix_s[rows, RET_WIDTH:], wout_ref[RET_WIDTH:, :])
            xo_ref[rows, :] = x_ref[rows, :] + y

        return [[first], [second]]

    def phase(bigs, items):
        tasks = []
        n_slots = max(len(bigs), -(-len(items) // _SMALL_TASK_BATCH))
        per = -(-len(items) // n_slots) if items else 0
        pending = []
        for slot in range(n_slots):
            if slot < len(bigs):
                tasks += bigs[slot]
            batch = items[slot * per:(slot + 1) * per]
            tasks += [scores for scores, _ in batch]
            tasks += pending
            pending = [task for _, later in batch for task in later]
        return tasks + pending

    program = []
    for p in range(n_sub + 2):
        bigs = _interleave(projection_tasks(p) if p < n_sub else [],
                           output_tasks(p - 2) if 0 <= p - 2 < n_sub else [])
        program += phase(bigs, chunk_items(p - 1) if 0 <= p - 1 < n_sub else [])
    for task in program:
        task()


def _ffn_kernel(x_ref, g_ref, wup_ref, wdn_ref, gfin_ref, o_ref, hid_s, *, n_sub, final):
    M = x_ref.shape[0] // n_sub
    n_col = D_FF // _FFN_COL_CHUNK

    def group_tasks(j):
        rows = slice(j * M, (j + 1) * M)
        val = {}

        def norm():
            val['h'] = _rms_scale(x_ref[rows, :], g_ref[...]).astype(_BF16)

        def up(n):
            def task():
                cols = slice(n * _FFN_COL_CHUNK, (n + 1) * _FFN_COL_CHUNK)
                a = jnp.maximum(_dot(val['h'], wup_ref[:, cols]), 0.0)
                hid_s[rows, cols] = (a * a).astype(_BF16)
            return task

        def down():
            y = x_ref[rows, :] + _dot(hid_s[rows, :], wdn_ref[...])
            if final:
                y = _rms_scale(y, gfin_ref[...])
            o_ref[rows, :] = y

        return norm, [up(n) for n in range(n_col)] + [down]

    groups = [group_tasks(j) for j in range(n_sub)]
    groups[0][0]()
    for j, (_, matmuls) in enumerate(groups):
        for n, task in enumerate(matmuls):
            task()
            if n == 0 and j + 1 < n_sub:
                groups[j + 1][0]()


def _log_gamma(hd):
    return math.log(1.0 - 2.0 ** (-5.0 - hd))


def _retention_constants(C):
    lg = jnp.asarray([_log_gamma(hd) for hd in range(RET_HEADS)], _F32)
    idx = jnp.arange(C, dtype=_F32)
    diff = idx[:, None] - idx[None, :]
    decay = jnp.where(diff >= 0, jnp.exp(lg[:, None, None] * jnp.maximum(diff, 0.0)), 0.0)
    cross = jnp.exp(lg[:, None] * (idx[None, :] + 1.0))
    kdec = jnp.exp(lg[:, None] * (C - 1.0 - idx[None, :]))
    bcast = lambda a: jnp.broadcast_to(a[:, :, None], (RET_HEADS, C, HEAD_DIM))
    kdec_t = jnp.broadcast_to(kdec[:, None, :], (RET_HEADS, HEAD_DIM, C))
    return decay * Q_SCALE, bcast(cross * Q_SCALE), kdec_t


def _rope_tables(pos):
    half = HEAD_DIM // 2
    inv_freq = ROPE_BASE ** (-jnp.arange(half, dtype=_F32) / half)
    ang = pos[:, None] * inv_freq[None, :]
    cos, sin = jnp.cos(ang), jnp.sin(ang)
    return jnp.concatenate([cos, cos], axis=-1), jnp.concatenate([-sin, sin], axis=-1)


def _const_spec(shape):
    return pl.BlockSpec(shape, lambda *_: (0,) * len(shape), pipeline_mode=pl.Buffered(1))


def _layer_spec(shape, layer):
    return pl.BlockSpec((None,) + shape, lambda *_: (layer, 0, 0), pipeline_mode=pl.Buffered(1))


def _mixer(x2d, s0, acc, lw, tables, *, layer, n_streams, S, nb, T, C, group_rows):
    has_state = s0 is not None
    want_gv = len(acc) == 2
    assert S % T == 0 and T % C == 0 and n_streams % nb == 0
    assert C == min(S, GM_CHUNK)
    assert nb == 1 or S == T
    R = nb * T
    n_t = S // T
    cos, sin, decay, cross, kdec = tables
    gmw = lw['gm_w'][:, :C, :C]
    gmb = jnp.broadcast_to(lw['gm_b'][:, :C, None], (GM_GROUPS, C, HEAD_DIM))

    grid = (n_streams // nb, n_t)
    row_spec = lambda width: pl.BlockSpec((R, width), lambda b, t: (b * n_t + t, 0))
    state_spec = pl.BlockSpec((None, nb, RET_HEADS, HEAD_DIM, HEAD_DIM),
                              lambda b, t: (layer, b, 0, 0, 0))
    rope_spec = pl.BlockSpec((R, HEAD_DIM), lambda b, t: (t, 0))
    in_specs = [pl.BlockSpec((R, D_MODEL), lambda b, t: (b * n_t + t, 0),
                             pipeline_mode=pl.Buffered(_ACTIVATION_BUFFERS))]
    args = [x2d]
    if has_state:
        in_specs.append(state_spec)
        args.append(s0)
    in_specs += [
        _const_spec((1, D_MODEL)),
        _layer_spec((D_MODEL, 6 * RET_WIDTH), layer),
        rope_spec,
        rope_spec,
        _const_spec((RET_HEADS, C, C)),
        _const_spec((RET_HEADS, C, HEAD_DIM)),
        _const_spec((RET_HEADS, HEAD_DIM, C)),
        _const_spec((1, RET_WIDTH)),
        _const_spec((1, GM_WIDTH)),
        _const_spec((1, GM_WIDTH)),
        _const_spec((GM_GROUPS, C, C)),
        _const_spec((GM_GROUPS, C, HEAD_DIM)),
        _layer_spec((D_MODEL, D_MODEL), layer),
    ]
    args += [lw['g_mix'], lw['w_in'], cos, sin, decay, cross, kdec, lw['ret_gn'],
             lw['gm_ln_g'], lw['gm_ln_b'], gmw, gmb, lw['w_out']]
    first_acc = len(args)
    in_specs += [pl.BlockSpec(memory_space=pl.ANY)] * len(acc)
    args += list(acc)

    out_shape = [jax.ShapeDtypeStruct(x2d.shape, _F32)]
    out_shape += [jax.ShapeDtypeStruct(a.shape, _F32) for a in acc]
    out_specs = [row_spec(D_MODEL), state_spec]
    if want_gv:
        out_specs.append(pl.BlockSpec((None, R, GM_WIDTH), lambda b, t: (layer, b * n_t + t, 0)))

    scratch = [pltpu.VMEM((R, RET_WIDTH), _BF16),
               pltpu.VMEM((R, RET_WIDTH), _BF16),
               pltpu.VMEM((R, RET_WIDTH), _BF16),
               pltpu.VMEM((R, RET_WIDTH), _F32),
               pltpu.VMEM((R, GM_WIDTH), _F32),
               pltpu.VMEM((R, GM_WIDTH), _BF16),
               pltpu.VMEM((R, D_MODEL), _BF16)]

    kern = functools.partial(_mixer_kernel, nb=nb, T=T, C=C, group_rows=group_rows,
                             has_state=has_state, want_gv=want_gv)
    return pl.pallas_call(
        kern,
        grid=grid,
        in_specs=in_specs,
        out_specs=out_specs,
        out_shape=out_shape,
        scratch_shapes=scratch,
        input_output_aliases={first_acc + n: 1 + n for n in range(len(acc))},
        compiler_params=pltpu.CompilerParams(
            dimension_semantics=("arbitrary",) * len(grid),
            vmem_limit_bytes=_VMEM_LIMIT_BYTES),
        name="mixer_state" if has_state else "mixer",
    )(*args)


def _ffn(x2d, lw, g_final, *, layer, R, final):
    n_rows = x2d.shape[0]
    assert n_rows % R == 0
    n_sub = max(1, R // _FFN_GROUP_ROWS)
    const = lambda shape: pl.BlockSpec(shape, lambda r: (0,) * len(shape),
                                       pipeline_mode=pl.Buffered(1))
    row_spec = pl.BlockSpec((R, D_MODEL), lambda r: (r, 0))
    return pl.pallas_call(
        functools.partial(_ffn_kernel, n_sub=n_sub, final=final),
        grid=(n_rows // R,),
        in_specs=[pl.BlockSpec((R, D_MODEL), lambda r: (r, 0),
                               pipeline_mode=pl.Buffered(_ACTIVATION_BUFFERS)),
                  const((1, D_MODEL)), _layer_spec((D_MODEL, D_FF), layer),
                  _layer_spec((D_FF, D_MODEL), layer), const((1, D_MODEL))],
        out_specs=row_spec,
        out_shape=jax.ShapeDtypeStruct(x2d.shape, _F32),
        scratch_shapes=[pltpu.VMEM((R, D_FF), _BF16)],
        compiler_params=pltpu.CompilerParams(
            dimension_semantics=("arbitrary",),
            allow_input_fusion=[False, False, True, True, False],
            vmem_limit_bytes=_VMEM_LIMIT_BYTES),
        name="ffn_final" if final else "ffn",
    )(x2d, lw['g_ffn'], lw['w_up'], lw['w_down'], g_final)


def _pick_tile(S, pref):
    t = min(S, pref)
    while S % t:
        t //= 2
    return t


def _forward(x_prompt, x_sample, state_ret, g_mix, w_in, ret_gn, gm_ln_g, gm_ln_b, gm_w, gm_b,
             w_out, g_ffn, w_up, w_down, g_final):
    B, S, D = x_prompt.shape
    DB, DS, _ = x_sample.shape
    depth = w_in.shape[0]

    T_p = _pick_tile(S, _PROMPT_TILE_ROWS)
    C_p = min(S, GM_CHUNK)
    groups_p = _PROMPT_GROUP_ROWS if sum(_PROMPT_GROUP_ROWS) == T_p else (T_p,)
    nb_s = math.gcd(DB, _SAMPLE_STREAMS_PER_TILE)
    groups_s = _SAMPLE_GROUP_ROWS if sum(_SAMPLE_GROUP_ROWS) == nb_s * DS else (nb_s * DS,)

    cos_p, sin_p = _rope_tables(jnp.arange(S, dtype=_F32))
    cos_s, sin_s = _rope_tables(PAST_LEN + jnp.arange(DS, dtype=_F32))
    cos_s, sin_s = jnp.tile(cos_s, (nb_s, 1)), jnp.tile(sin_s, (nb_s, 1))
    tab_p = (cos_p, sin_p) + _retention_constants(C_p)
    tab_s = (cos_s, sin_s) + _retention_constants(DS)

    row = lambda a: a.reshape(1, -1)
    g_fin = row(g_final)
    hp = x_prompt.reshape(B * S, D)
    hs = x_sample.reshape(DB * DS, D)
    R_ffn_p = _pick_tile(B * S, _FFN_TILE_ROWS)
    R_ffn_s = _pick_tile(DB * DS, _FFN_TILE_ROWS)

    w_in_b, w_out_b = w_in.astype(_BF16), w_out.astype(_BF16)
    w_up_b, w_down_b = w_up.astype(_BF16), w_down.astype(_BF16)

    state_p = jnp.zeros((depth, B, RET_HEADS, HEAD_DIM, HEAD_DIM), _F32)
    state_s = jnp.zeros((depth, DB, RET_HEADS, HEAD_DIM, HEAD_DIM), _F32)
    gv_s = jnp.zeros((depth, DB * DS, GM_WIDTH), _F32)
    for l in range(depth):
        lw = dict(g_mix=row(g_mix[l]), w_in=w_in_b, ret_gn=row(ret_gn[l]),
                  gm_ln_g=row(gm_ln_g[l]), gm_ln_b=row(gm_ln_b[l]), gm_w=gm_w[l], gm_b=gm_b[l],
                  w_out=w_out_b, g_ffn=row(g_ffn[l]), w_up=w_up_b, w_down=w_down_b)
        final = l == depth - 1
        hp, state_p = _mixer(hp, None, (state_p,), lw, tab_p, layer=l, n_streams=B, S=S, nb=1,
                             T=T_p, C=C_p, group_rows=groups_p)
        hp = _ffn(hp, lw, g_fin, layer=l, R=R_ffn_p, final=final)
        hs, state_s, gv_s = _mixer(hs, state_ret, (state_s, gv_s), lw, tab_s, layer=l,
                                   n_streams=DB, S=DS, nb=nb_s, T=DS, C=DS,
                                   group_rows=groups_s)
        hs = _ffn(hs, lw, g_fin, layer=l, R=R_ffn_s, final=final)

    return (hp.reshape(B, S, D), hs.reshape(DB, DS, D), state_p, state_s,
            gv_s.reshape(depth, DB, DS, GM_WIDTH))


def kernel(x_prompt, x_sample, state_ret, g_mix, w_in, ret_gn, gm_ln_g, gm_ln_b, gm_w, gm_b,
           w_out, g_ffn, w_up, w_down, g_final):
    return _forward(x_prompt, x_sample, state_ret, g_mix, w_in, ret_gn, gm_ln_g, gm_ln_b, gm_w,
                    gm_b, w_out, g_ffn, w_up, w_down, g_final)
```
